```python
import math
import jax, jax.numpy as jnp
from jax import lax
import numpy as np

D_MODEL = 4096
BATCH = 1
SEQ = 8192
DEPTH = 4

CTX_LEN = 256
GRID_W = 64
HEAD_DIM = 128
ROPE_THETA = 10000.0
Q_BLOCK = 128
LN_EPS = 1e-5
RMS_EPS = 1e-6

DIFF_HEADS = (D_MODEL // 2) // (2 * HEAD_DIM)
DIFF_V_DIM = 2 * HEAD_DIM
DIFF_QK = DIFF_HEADS * 2 * HEAD_DIM
DIFF_V = DIFF_HEADS * DIFF_V_DIM
GQA_HEADS = (D_MODEL // 2) // HEAD_DIM
GQA_KV_HEADS = GQA_HEADS // 4
GQA_Q = GQA_HEADS * HEAD_DIM
GQA_KV = GQA_KV_HEADS * HEAD_DIM
EVEN_SPLITS = [DIFF_QK, 2 * DIFF_QK, 2 * DIFF_QK + DIFF_V,
               2 * DIFF_QK + DIFF_V + GQA_Q, 2 * DIFF_QK + DIFF_V + GQA_Q + GQA_KV]
EVEN_IN = 2 * DIFF_QK + DIFF_V + GQA_Q + 2 * GQA_KV
EVEN_OUT = DIFF_V + GQA_Q

MLA_HEADS = D_MODEL // HEAD_DIM
MLA_Q_RANK = D_MODEL // 4
MLA_KV_RANK = D_MODEL // 8
MLA_NOPE = 128
MLA_ROPE = 64
MLA_V = 128
ODD_IN = MLA_Q_RANK + MLA_KV_RANK + MLA_ROPE
ODD_OUT = MLA_HEADS * MLA_V

N_GROUPS = 4
EXPERTS_PER_GROUP = 4
N_EXPERTS = N_GROUPS * EXPERTS_PER_GROUP
EXPERT_TOP_K = 2
D_EXPERT = 384

N_MOD = 6
DEEPNORM_ALPHA = (2.0 * DEPTH) ** 0.25
DEEPNORM_BETA = (8.0 * DEPTH) ** -0.25
N_EVEN = (DEPTH + 1) // 2
N_ODD = DEPTH // 2

kernel_name = "hybrid_diffgqa_mla_hiermoe_dit"


def _layer_norm(x, g, b):
    xf = x.astype(jnp.float32)
    mu = jnp.mean(xf, axis=-1, keepdims=True)
    var = jnp.mean(jnp.square(xf - mu), axis=-1, keepdims=True)
    y = (xf - mu) * lax.rsqrt(var + LN_EPS)
    return (y * g + b).astype(x.dtype)


def _rms_norm(x, g):
    xf = x.astype(jnp.float32)
    y = xf * lax.rsqrt(jnp.mean(jnp.square(xf), axis=-1, keepdims=True) + RMS_EPS)
    return (y * g).astype(x.dtype)


def _axial_rope_tables(n_tokens, rot_dim):
    rows = n_tokens // GRID_W
    row, col = jnp.meshgrid(jnp.arange(rows), jnp.arange(GRID_W), indexing="ij")
    row = row.reshape(-1).astype(jnp.float32)
    col = col.reshape(-1).astype(jnp.float32)
    quarter = rot_dim // 4
    inv_freq = ROPE_THETA ** (-jnp.arange(quarter, dtype=jnp.float32) / quarter)
    ang = jnp.concatenate([row[:, None] * inv_freq, col[:, None] * inv_freq], axis=-1)
    return jnp.cos(ang), jnp.sin(ang)


def _apply_rope(x, cos, sin):
    half = x.shape[-1] // 2
    xf = x.astype(jnp.float32)
    x1, x2 = xf[..., :half], xf[..., half:]
    c, s = cos[None, :, None, :], sin[None, :, None, :]
    return jnp.concatenate([x1 * c - x2 * s, x1 * s + x2 * c], axis=-1).astype(x.dtype)


def _block_attention(q, k, v, scale):
    B, S, H, d = q.shape
    G = k.shape[2]
    rep = H // G
    nb = S // Q_BLOCK
    qb = q.reshape(B, nb, Q_BLOCK, G, rep, d).transpose(1, 0, 2, 3, 4, 5)

    def one_block(q_blk):
        s = jnp.einsum("bqgrd,bkgd->bgrqk", q_blk, k, preferred_element_type=jnp.float32) * scale
        p = jax.nn.softmax(s, axis=-1).astype(v.dtype)
        return jnp.einsum("bgrqk,bkgv->bqgrv", p, v)

    o = lax.map(one_block, qb)
    return o.transpose(1, 0, 2, 3, 4, 5).reshape(B, S, H, v.shape[-1])


def _diff_lambda(lam_vecs, lambda_init):
    lv = lam_vecs.astype(jnp.float32)
    return jnp.exp(jnp.sum(lv[0] * lv[1])) - jnp.exp(jnp.sum(lv[2] * lv[3])) + lambda_init


def _diff_attention(q, k, v, lam, subln_g, lambda_init):
    B, S = q.shape[:2]
    scale = 1.0 / math.sqrt(HEAD_DIM)
    a1 = _block_attention(q[:, :, 0::2], k[:, :, 0::2], v, scale)
    a2 = _block_attention(q[:, :, 1::2], k[:, :, 1::2], v, scale)
    o = a1 - lam.astype(a1.dtype) * a2
    o = _rms_norm(o, subln_g) * (1.0 - lambda_init)
    return o.reshape(B, S, DIFF_V)


def _gqa(q, k, v):
    B, S = q.shape[:2]
    return _block_attention(q, k, v, 1.0 / math.sqrt(HEAD_DIM)).reshape(B, S, GQA_Q)


def _split_even(p):
    B, T = p.shape[:2]
    qa, ka, va, qg, kg, vg = jnp.split(p, EVEN_SPLITS, axis=-1)
    return (qa.reshape(B, T, 2 * DIFF_HEADS, HEAD_DIM),
            ka.reshape(B, T, 2 * DIFF_HEADS, HEAD_DIM),
            va.reshape(B, T, DIFF_HEADS, DIFF_V_DIM),
            qg.reshape(B, T, GQA_HEADS, HEAD_DIM),
            kg.reshape(B, T, GQA_KV_HEADS, HEAD_DIM),
            vg.reshape(B, T, GQA_KV_HEADS, HEAD_DIM))


def _even_mixer(h_l, h_c, w_in, w_out, lam_vecs, subln_g, qn_g, kn_g, lambda_init, cos, sin, need_ctx):
    qa_l, ka_l, va_l, qg_l, kg_l, vg_l = _split_even(h_l @ w_in)
    qa_c, ka_c, va_c, qg_c, kg_c, vg_c = _split_even(h_c @ w_in)
    lam = _diff_lambda(lam_vecs, lambda_init)
    qa_l = _apply_rope(qa_l, cos, sin)
    ka_l = _apply_rope(ka_l, cos, sin)
    qg_l = _apply_rope(_rms_norm(qg_l, qn_g), cos, sin)
    kg_l = _apply_rope(_rms_norm(kg_l, kn_g), cos, sin)
    kg_c = _rms_norm(kg_c, kn_g)
    ka_all = jnp.concatenate([ka_l, ka_c], axis=1)
    va_all = jnp.concatenate([va_l, va_c], axis=1)
    kg_all = jnp.concatenate([kg_l, kg_c], axis=1)
    vg_all = jnp.concatenate([vg_l, vg_c], axis=1)
    o_l = jnp.concatenate([_diff_attention(qa_l, ka_all, va_all, lam, subln_g, lambda_init),
                           _gqa(qg_l, kg_all, vg_all)], axis=-1) @ w_out
    o_c = None
    if need_ctx:
        o_c = jnp.concatenate([_diff_attention(qa_c, ka_c, va_c, lam, subln_g, lambda_init),
                               _gqa(_rms_norm(qg_c, qn_g), kg_c, vg_c)], axis=-1) @ w_out
    return o_l, o_c


def _mla_project(h, w_in, qn_g, kvn_g, w_uq, w_ukv):
    B, T = h.shape[:2]
    c_q, c_kv, k_rope = jnp.split(h @ w_in, [MLA_Q_RANK, MLA_Q_RANK + MLA_KV_RANK], axis=-1)
    q = (_rms_norm(c_q, qn_g) @ w_uq).reshape(B, T, MLA_HEADS, MLA_NOPE + MLA_ROPE)
    kv = (_rms_norm(c_kv, kvn_g) @ w_ukv).reshape(B, T, MLA_HEADS, MLA_NOPE + MLA_V)
    return (q[..., :MLA_NOPE], q[..., MLA_NOPE:], kv[..., :MLA_NOPE], kv[..., MLA_NOPE:],
            k_rope.reshape(B, T, 1, MLA_ROPE))


def _mla_keys(k_nope, k_rope):
    return jnp.concatenate([k_nope, jnp.broadcast_to(k_rope, k_nope.shape[:-1] + (MLA_ROPE,))], axis=-1)


def _odd_mixer(h_l, h_c, w_in, qn_g, kvn_g, w_uq, w_ukv, w_out, cos, sin, need_ctx):
    B, S = h_l.shape[:2]
    scale = 1.0 / math.sqrt(MLA_NOPE + MLA_ROPE)
    qn_l, qr_l, kn_l, v_l, kr_l = _mla_project(h_l, w_in, qn_g, kvn_g, w_uq, w_ukv)
    qn_c, qr_c, kn_c, v_c, kr_c = _mla_project(h_c, w_in, qn_g, kvn_g, w_uq, w_ukv)
    q_l = jnp.concatenate([qn_l, _apply_rope(qr_l, cos, sin)], axis=-1)
    k_l = _mla_keys(kn_l, _apply_rope(kr_l, cos, sin))
    k_c = _mla_keys(kn_c, kr_c)
    o_l = _block_attention(q_l, jnp.concatenate([k_l, k_c], axis=1),
                           jnp.concatenate([v_l, v_c], axis=1), scale).reshape(B, S, ODD_OUT) @ w_out
    o_c = None
    if need_ctx:
        q_c = jnp.concatenate([qn_c, qr_c], axis=-1)
        o_c = _block_attention(q_c, k_c, v_c, scale).reshape(B, h_c.shape[1], ODD_OUT) @ w_out
    return o_l, o_c


def _hier_moe(x, w_group, b_group, w_router, b_router, w_gate, w_up, w_down):
    B, T, D = x.shape
    xf = x.reshape(B * T, D)
    group_prob = jax.nn.softmax((xf @ w_group + b_group).astype(jnp.float32), axis=-1)
    group_idx = jnp.argmax(group_prob, axis=-1)
    group_w = jnp.take_along_axis(group_prob, group_idx[:, None], axis=-1)
    e_logits = (xf @ w_router + b_router).astype(jnp.float32).reshape(-1, N_GROUPS, EXPERTS_PER_GROUP)
    in_group = jnp.take_along_axis(e_logits, group_idx[:, None, None], axis=1)[:, 0]
    top_p, top_i = lax.top_k(jax.nn.softmax(in_group, axis=-1), EXPERT_TOP_K)
    top_p = top_p / jnp.sum(top_p, axis=-1, keepdims=True)
    expert_id = group_idx[:, None] * EXPERTS_PER_GROUP + top_i
    gates = jnp.einsum("nk,nke->ne", group_w * top_p,
                       jax.nn.one_hot(expert_id, N_EXPERTS, dtype=jnp.float32))
    hidden = jax.nn.silu(jnp.einsum("nd,edf->nef", xf, w_gate)) * jnp.einsum("nd,edf->nef", xf, w_up)
    y = jnp.einsum("nef,efd->nd", hidden * gates[:, :, None].astype(hidden.dtype), w_down)
    return y.reshape(B, T, D)


def setup_inputs(seed: int = 0) -> dict:
    key = jax.random.key(seed)
    ks = jax.random.split(key, 32)

    def nrm(k, shape, scale):
        return jax.random.normal(k, shape, jnp.float32) * scale

    D = D_MODEL
    return {
        "x": nrm(ks[0], (BATCH, SEQ, D), 1.0),
        "c": nrm(ks[1], (BATCH, D), 1.0),
        "ctx": nrm(ks[2], (BATCH, CTX_LEN, D), 1.0),
        "c_ctx": nrm(ks[3], (D,), 1.0),
        "w_ada": nrm(ks[4], (DEPTH, D, N_MOD * D), 0.5 * D ** -0.5),
        "b_ada": nrm(ks[5], (DEPTH, N_MOD * D), 0.01),
        "ln1_g": 1.0 + nrm(ks[6], (DEPTH, D), 0.02),
        "ln1_b": nrm(ks[7], (DEPTH, D), 0.02),
        "ln2_g": 1.0 + nrm(ks[8], (DEPTH, D), 0.02),
        "ln2_b": nrm(ks[9], (DEPTH, D), 0.02),
        "ev_w_in": nrm(ks[10], (N_EVEN, D, EVEN_IN), D ** -0.5),
        "ev_w_out": nrm(ks[11], (N_EVEN, EVEN_OUT, D), EVEN_OUT ** -0.5 * DEEPNORM_BETA),
        "diff_lambda": nrm(ks[12], (N_EVEN, 4, HEAD_DIM), 0.1),
        "diff_subln_g": 1.0 + nrm(ks[13], (N_EVEN, DIFF_V_DIM), 0.02),
        "gqa_q_norm_g": 1.0 + nrm(ks[14], (N_EVEN, HEAD_DIM), 0.02),
        "gqa_k_norm_g": 1.0 + nrm(ks[15], (N_EVEN, HEAD_DIM), 0.02),
        "od_w_in": nrm(ks[16], (N_ODD, D, ODD_IN), D ** -0.5),
        "mla_q_norm_g": 1.0 + nrm(ks[17], (N_ODD, MLA_Q_RANK), 0.02),
        "mla_kv_norm_g": 1.0 + nrm(ks[18], (N_ODD, MLA_KV_RANK), 0.02),
        "mla_w_uq": nrm(ks[19], (N_ODD, MLA_Q_RANK, MLA_HEADS * (MLA_NOPE + MLA_ROPE)), MLA_Q_RANK ** -0.5),
        "mla_w_ukv": nrm(ks[20], (N_ODD, MLA_KV_RANK, MLA_HEADS * (MLA_NOPE + MLA_V)), MLA_KV_RANK ** -0.5),
        "od_w_out": nrm(ks[21], (N_ODD, ODD_OUT, D), ODD_OUT ** -0.5 * DEEPNORM_BETA),
        "moe_w_group": nrm(ks[22], (DEPTH, D, N_GROUPS), D ** -0.5),
        "moe_b_group": nrm(ks[23], (DEPTH, N_GROUPS), 0.01),
        "moe_w_router": nrm(ks[24], (DEPTH, D, N_EXPERTS), D ** -0.5),
        "moe_b_router": nrm(ks[25], (DEPTH, N_EXPERTS), 0.01),
        "moe_w_gate": nrm(ks[26], (DEPTH, N_EXPERTS, D, D_EXPERT), D ** -0.5),
        "moe_w_up": nrm(ks[27], (DEPTH, N_EXPERTS, D, D_EXPERT), D ** -0.5),
        "moe_w_down": nrm(ks[28], (DEPTH, N_EXPERTS, D_EXPERT, D), D_EXPERT ** -0.5 * DEEPNORM_BETA),
    }


def reference(x, c, ctx, c_ctx, w_ada, b_ada, ln1_g, ln1_b, ln2_g, ln2_b,
              ev_w_in, ev_w_out, diff_lambda, diff_subln_g, gqa_q_norm_g, gqa_k_norm_g,
              od_w_in, mla_q_norm_g, mla_kv_norm_g, mla_w_uq, mla_w_ukv, od_w_out,
              moe_w_group, moe_b_group, moe_w_router, moe_b_router, moe_w_gate, moe_w_up, moe_w_down):
    n_lat = x.shape[1]
    cos_a, sin_a = _axial_rope_tables(n_lat, HEAD_DIM)
    cos_m, sin_m = _axial_rope_tables(n_lat, MLA_ROPE)
    silu_c = jax.nn.silu(c)
    silu_cc = jax.nn.silu(c_ctx)
    xl, xc = x, ctx
    for layer in range(DEPTH):
        need_ctx = layer < DEPTH - 1
        mods_l = [m[:, None, :] for m in jnp.split(silu_c @ w_ada[layer] + b_ada[layer], N_MOD, axis=-1)]
        mods_c = jnp.split(silu_cc @ w_ada[layer] + b_ada[layer], N_MOD, axis=-1)
        sh1_l, sc1_l, g1_l, sh2_l, sc2_l, g2_l = mods_l
        sh1_c, sc1_c, g1_c, sh2_c, sc2_c, g2_c = mods_c
        h_l = xl * (1.0 + sc1_l) + sh1_l
        h_c = xc * (1.0 + sc1_c) + sh1_c
        i = layer // 2
        if layer % 2 == 0:
            lambda_init = 0.8 - 0.6 * math.exp(-0.3 * layer)
            o_l, o_c = _even_mixer(h_l, h_c, ev_w_in[i], ev_w_out[i], diff_lambda[i], diff_subln_g[i],
                                   gqa_q_norm_g[i], gqa_k_norm_g[i], lambda_init, cos_a, sin_a, need_ctx)
        else:
            o_l, o_c = _odd_mixer(h_l, h_c, od_w_in[i], mla_q_norm_g[i], mla_kv_norm_g[i],
                                  mla_w_uq[i], mla_w_ukv[i], od_w_out[i], cos_m, sin_m, need_ctx)
        xl = _layer_norm(DEEPNORM_ALPHA * xl + g1_l * o_l, ln1_g[layer], ln1_b[layer])
        h2_l = xl * (1.0 + sc2_l) + sh2_l
        moe_args = (moe_w_group[layer], moe_b_group[layer], moe_w_router[layer], moe_b_router[layer],
                    moe_w_gate[layer], moe_w_up[layer], moe_w_down[layer])
        if need_ctx:
            xc = _layer_norm(DEEPNORM_ALPHA * xc + g1_c * o_c, ln1_g[layer], ln1_b[layer])
            h2_c = xc * (1.0 + sc2_c) + sh2_c
            n_ctx = xc.shape[1]
            f_all = _hier_moe(jnp.concatenate([h2_c, h2_l], axis=1), *moe_args)
            f_c, f_l = f_all[:, :n_ctx], f_all[:, n_ctx:]
            xc = _layer_norm(DEEPNORM_ALPHA * xc + g2_c * f_c, ln2_g[layer], ln2_b[layer])
        else:
            f_l = _hier_moe(h2_l, *moe_args)
        xl = _layer_norm(DEEPNORM_ALPHA * xl + g2_l * f_l, ln2_g[layer], ln2_b[layer])
    return xl
```

```python
import functools
import math

import jax
import jax.numpy as jnp
from jax import lax
from jax.experimental import pallas as pl
from jax.experimental.pallas import tpu as pltpu

F32 = jnp.float32
BF16 = jnp.bfloat16

GRID_W = 64
HEAD_DIM = 128
ROPE_THETA = 10000.0
LN_EPS = 1e-5
RMS_EPS = 1e-6
N_GROUPS = 4
EXPERTS_PER_GROUP = 4
N_EXPERTS = N_GROUPS * EXPERTS_PER_GROUP
N_MOD = 6
MLA_NOPE = 128
MLA_ROPE = 64
MLA_V = 128
MLA_HEAD_PAD = 256

LANES = 128
V7X_VMEM_LIMIT = 56 * 2**20

ROW_TILE = 256
MM_TM = 768
KV_CHUNK = 768


def _cparams(*sem):
    return pltpu.CompilerParams(dimension_semantics=sem, vmem_limit_bytes=V7X_VMEM_LIMIT)


def _ada_kernel(ct_ref, w_ref, b_ref, o_ref):
    ct = ct_ref[...]
    s = ct * jax.nn.sigmoid(ct)
    w = w_ref[...]
    b = b_ref[...]
    for r in range(2):
        o_ref[r:r + 1, :] = jnp.sum(w * s[:, r:r + 1], axis=0, keepdims=True) + b


def _ada_mods(c, c_ctx, w_ada, b_ada):
    depth, d, n = w_ada.shape
    tn = 512
    ct = jnp.stack([c.reshape(d), c_ctx.reshape(d)], axis=1)
    return pl.pallas_call(
        _ada_kernel,
        grid=(depth, n // tn),
        in_specs=[
            pl.BlockSpec((d, 2), lambda l, j: (0, 0)),
            pl.BlockSpec((None, d, tn), lambda l, j: (l, 0, j)),
            pl.BlockSpec((None, 1, tn), lambda l, j: (l, 0, j)),
        ],
        out_specs=pl.BlockSpec((None, 2, tn), lambda l, j: (l, 0, j)),
        out_shape=jax.ShapeDtypeStruct((depth, 2, n), F32),
        compiler_params=_cparams("arbitrary", "arbitrary"),
        name="ada_mods",
    )(ct, w_ada, b_ada.reshape(depth, 1, n))


def _mod_spec(layer, which, d, seq):
    def index(i):
        sel = jnp.where(i * ROW_TILE >= seq, 1, 0)
        return ((layer * 2 + sel) * N_MOD + which, 0, 0)
    return pl.BlockSpec((None, 1, d), index)


def _modulate_kernel(x_ref, sc_ref, sh_ref, h_ref):
    h_ref[...] = (x_ref[...] * (1.0 + sc_ref[...]) + sh_ref[...]).astype(h_ref.dtype)


def _modulate(x, mods3, layer, sc_idx, sh_idx, seq):
    m, d = x.shape
    return pl.pallas_call(
        _modulate_kernel,
        grid=(m // ROW_TILE,),
        in_specs=[pl.BlockSpec((ROW_TILE, d), lambda i: (i, 0)),
                  _mod_spec(layer, sc_idx, d, seq), _mod_spec(layer, sh_idx, d, seq)],
        out_specs=pl.BlockSpec((ROW_TILE, d), lambda i: (i, 0)),
        out_shape=jax.ShapeDtypeStruct((m, d), BF16),
        compiler_params=_cparams("arbitrary"),
        name="modulate",
    )(x, mods3, mods3)


def _resid_ln_kernel(x_ref, o_ref, g_ref, lg_ref, lb_ref, sc_ref, sh_ref, y_ref, h_ref, *, alpha):
    z = alpha * x_ref[...] + g_ref[...] * o_ref[...]
    mu = jnp.mean(z, axis=-1, keepdims=True)
    zc = z - mu
    var = jnp.mean(zc * zc, axis=-1, keepdims=True)
    y = zc * lax.rsqrt(var + LN_EPS) * lg_ref[...] + lb_ref[...]
    y_ref[...] = y
    h_ref[...] = (y * (1.0 + sc_ref[...]) + sh_ref[...]).astype(h_ref.dtype)


def _resid_ln(x, o, mods3, layer, gate_idx, ln_g, ln_b, nxt_layer, sc_idx, sh_idx, seq, alpha):
    m, d = x.shape
    row = pl.BlockSpec((ROW_TILE, d), lambda i: (i, 0))
    vec = pl.BlockSpec((1, d), lambda i: (0, 0))
    return pl.pallas_call(
        functools.partial(_resid_ln_kernel, alpha=alpha),
        grid=(m // ROW_TILE,),
        in_specs=[row, row, _mod_spec(layer, gate_idx, d, seq), vec, vec,
                  _mod_spec(nxt_layer, sc_idx, d, seq), _mod_spec(nxt_layer, sh_idx, d, seq)],
        out_specs=[row, row],
        out_shape=[jax.ShapeDtypeStruct((m, d), F32), jax.ShapeDtypeStruct((m, d), BF16)],
        compiler_params=_cparams("arbitrary"),
        name="resid_ln",
    )(x, o, mods3, ln_g.reshape(1, d), ln_b.reshape(1, d), mods3, mods3)


def _mm_kernel(*refs, epilogue, n_extra):
    a_ref, b_ref = refs[0], refs[1]
    extras = refs[2:2 + n_extra]
    o_ref = refs[2 + n_extra]
    acc = jnp.dot(a_ref[...], b_ref[...], preferred_element_type=F32)
    epilogue(acc, o_ref, *extras)


def _store(acc, o_ref):
    o_ref[...] = acc.astype(o_ref.dtype)


def _matmul(a, b, out_dtype, *, tn, epilogue=_store, extras=(), extra_specs=(), name="matmul"):
    m, k = a.shape
    n = b.shape[1]
    tm = MM_TM
    tn = min(tn, n)
    assert m % tm == 0 and n % tn == 0
    return pl.pallas_call(
        functools.partial(_mm_kernel, epilogue=epilogue, n_extra=len(extras)),
        grid=(n // tn, m // tm),
        in_specs=[pl.BlockSpec((tm, k), lambda j, i: (i, 0)),
                  pl.BlockSpec((k, tn), lambda j, i: (0, j)),
                  *extra_specs],
        out_specs=pl.BlockSpec((tm, tn), lambda j, i: (i, j)),
        out_shape=jax.ShapeDtypeStruct((m, n), out_dtype),
        compiler_params=_cparams("arbitrary", "arbitrary"),
        name=name,
    )(a, b, *extras)


def _rope_tables(seq, n_ctx, rot_dim):
    rows = seq // GRID_W
    row, col = jnp.meshgrid(jnp.arange(rows), jnp.arange(GRID_W), indexing="ij")
    row = row.reshape(-1).astype(F32)
    col = col.reshape(-1).astype(F32)
    quarter = rot_dim // 4
    inv_freq = ROPE_THETA ** (-jnp.arange(quarter, dtype=F32) / quarter)
    ang = jnp.concatenate([row[:, None] * inv_freq, col[:, None] * inv_freq], axis=-1)
    cos, sin = jnp.cos(ang), jnp.sin(ang)
    half = rot_dim // 2
    pad = LANES - rot_dim
    zeros_h = jnp.zeros((seq, half), F32)
    zeros_p = jnp.zeros((seq, pad), F32)
    cos_t = jnp.concatenate([cos, cos, zeros_p], axis=-1)
    sin_lo = jnp.concatenate([-sin, zeros_h, zeros_p], axis=-1)
    sin_hi = jnp.concatenate([zeros_h, sin, zeros_p], axis=-1)
    ident = jnp.concatenate([jnp.ones((n_ctx, rot_dim), F32), jnp.zeros((n_ctx, pad), F32)], axis=-1)
    zc = jnp.zeros((n_ctx, LANES), F32)
    return (jnp.concatenate([cos_t, ident], axis=0), jnp.concatenate([sin_lo, zc], axis=0),
            jnp.concatenate([sin_hi, zc], axis=0))


def _rope_block(x, cos, sin_lo, sin_hi, half):
    return (x * cos + pltpu.roll(x, LANES - half, axis=1) * sin_lo
            + pltpu.roll(x, half, axis=1) * sin_hi)


def _head_epilogue(acc, o_ref, *extras, norm, rope, scale):
    idx = 0
    if norm:
        g = extras[idx][...]
        idx += 1
    if rope:
        cos, sin_lo, sin_hi = (r[...] for r in extras[idx:idx + 3])
    for hb in range(acc.shape[1] // LANES):
        x = acc[:, hb * LANES:(hb + 1) * LANES]
        if norm:
            x = x * lax.rsqrt(jnp.mean(x * x, axis=-1, keepdims=True) + RMS_EPS) * g
        if rope:
            x = _rope_block(x, cos, sin_lo, sin_hi, HEAD_DIM // 2)
        if scale != 1.0:
            x = x * scale
        o_ref[:, hb * LANES:(hb + 1) * LANES] = x.astype(o_ref.dtype)


def _head_proj(h, w, *, tn, norm_g=None, tables=None, scale=1.0, name):
    extras, specs = [], []
    if norm_g is not None:
        extras.append(norm_g.reshape(1, LANES))
        specs.append(pl.BlockSpec((1, LANES), lambda j, i: (0, 0)))
    if tables is not None:
        extras.extend(tables)
        specs.extend([pl.BlockSpec((MM_TM, LANES), lambda j, i: (i, 0))] * 3)
    ep = functools.partial(_head_epilogue, norm=norm_g is not None, rope=tables is not None, scale=scale)
    return _matmul(h, w, BF16, tn=tn, epilogue=ep, extras=extras, extra_specs=specs, name=name)


def _rmsnorm_epilogue(acc, o_ref, g_ref):
    x = acc * lax.rsqrt(jnp.mean(acc * acc, axis=-1, keepdims=True) + RMS_EPS) * g_ref[...]
    o_ref[...] = x.astype(o_ref.dtype)


def _rope_m_epilogue(acc, o_ref, cos_ref, lo_ref, hi_ref):
    o_ref[...] = _rope_block(acc, cos_ref[...], lo_ref[...], hi_ref[...], MLA_ROPE // 2).astype(o_ref.dtype)


def _mla_q_epilogue(acc, o_ref, cos_ref, lo_ref, hi_ref, *, scale):
    cos, lo, hi = cos_ref[...], lo_ref[...], hi_ref[...]
    for hd in range(acc.shape[1] // MLA_HEAD_PAD):
        base = hd * MLA_HEAD_PAD
        o_ref[:, base:base + LANES] = (acc[:, base:base + LANES] * scale).astype(o_ref.dtype)
        r = _rope_block(acc[:, base + LANES:base + 2 * LANES], cos, lo, hi, MLA_ROPE // 2)
        o_ref[:, base + LANES:base + 2 * LANES] = (r * scale).astype(o_ref.dtype)


def _flash(q, k_ref, v_ref, m_ref, l_ref, acc_ref, tk):
    n_chunks = k_ref.shape[0] // tk
    m_ref[...] = jnp.full(m_ref.shape, -jnp.inf, F32)
    l_ref[...] = jnp.zeros(l_ref.shape, F32)
    acc_ref[...] = jnp.zeros(acc_ref.shape, F32)

    def body(c, carry):
        start = pl.multiple_of(c * tk, tk)
        k = k_ref[pl.ds(start, tk), :]
        s = lax.dot_general(q, k, (((1,), (1,)), ((), ())), preferred_element_type=F32)
        m_old = m_ref[...]
        m_new = jnp.maximum(m_old, jnp.max(s, axis=-1, keepdims=True))
        p = jnp.exp(s - m_new)
        alpha = jnp.exp(m_old - m_new)
        l_ref[...] = alpha * l_ref[...] + jnp.sum(p, axis=-1, keepdims=True)
        acc_ref[...] = alpha * acc_ref[...] + jnp.dot(
            p.astype(BF16), v_ref[pl.ds(start, tk), :], preferred_element_type=F32)
        m_ref[...] = m_new
        return carry

    lax.fori_loop(0, n_chunks, body, 0)
    return acc_ref[...] / l_ref[...]


def _diff_attn_kernel(lam_ref, g_ref, q1_ref, q2_ref, k1_ref, k2_ref, v_ref, o_ref,
                      m_ref, l_ref, acc_ref, *, tk, lambda_init):
    lv = lam_ref[...]
    lam = (jnp.exp(jnp.sum(lv[0:1] * lv[1:2], axis=-1, keepdims=True))
           - jnp.exp(jnp.sum(lv[2:3] * lv[3:4], axis=-1, keepdims=True)) + lambda_init)
    a1 = _flash(q1_ref[...], k1_ref, v_ref, m_ref, l_ref, acc_ref, tk)
    a2 = _flash(q2_ref[...], k2_ref, v_ref, m_ref, l_ref, acc_ref, tk)
    o = a1 - lam * a2
    o = o * lax.rsqrt(jnp.mean(o * o, axis=-1, keepdims=True) + RMS_EPS) * g_ref[...]
    o_ref[...] = (o * (1.0 - lambda_init)).astype(o_ref.dtype)


def _gqa_attn_kernel(q_ref, k_ref, v_ref, o_ref, m_ref, l_ref, acc_ref, *, tk, rep):
    tq = q_ref.shape[0]
    q = jnp.concatenate([q_ref[:, r * LANES:(r + 1) * LANES] for r in range(rep)], axis=0)
    o = _flash(q, k_ref, v_ref, m_ref, l_ref, acc_ref, tk)
    for r in range(rep):
        o_ref[:, r * LANES:(r + 1) * LANES] = o[r * tq:(r + 1) * tq].astype(o_ref.dtype)


def _mla_attn_kernel(q_ref, kn_ref, kr_ref, v_ref, o_ref, kcat_ref, m_ref, l_ref, acc_ref, *, tk):
    @pl.when(pl.program_id(1) == 0)
    def _():
        kcat_ref[:, :LANES] = kn_ref[...]
        kcat_ref[:, LANES:] = kr_ref[...]
    o_ref[...] = _flash(q_ref[...], kcat_ref, v_ref, m_ref, l_ref, acc_ref, tk).astype(o_ref.dtype)


def _flash_scratch(rows, dv):
    return [pltpu.VMEM((rows, 1), F32), pltpu.VMEM((rows, 1), F32), pltpu.VMEM((rows, dv), F32)]


def _diff_attention(qa, ka, va, lam_vecs, subln_g, lambda_init, *, q_rows, kv_rows, tq, tk):
    n_heads = va.shape[1] // (2 * HEAD_DIM)
    q0, nq = q_rows
    k0, nk = kv_rows
    qb, kb = q0 // tq, k0 // nk
    dv = 2 * HEAD_DIM
    return pl.pallas_call(
        functools.partial(_diff_attn_kernel, tk=tk, lambda_init=lambda_init),
        grid=(n_heads, nq // tq),
        in_specs=[
            pl.BlockSpec((4, HEAD_DIM), lambda h, i: (0, 0)),
            pl.BlockSpec((1, dv), lambda h, i: (0, 0)),
            pl.BlockSpec((tq, HEAD_DIM), lambda h, i: (qb + i, 2 * h)),
            pl.BlockSpec((tq, HEAD_DIM), lambda h, i: (qb + i, 2 * h + 1)),
            pl.BlockSpec((nk, HEAD_DIM), lambda h, i: (kb, 2 * h)),
            pl.BlockSpec((nk, HEAD_DIM), lambda h, i: (kb, 2 * h + 1)),
            pl.BlockSpec((nk, dv), lambda h, i: (kb, h)),
        ],
        out_specs=pl.BlockSpec((tq, dv), lambda h, i: (i, h)),
        out_shape=jax.ShapeDtypeStruct((nq, n_heads * dv), BF16),
        scratch_shapes=_flash_scratch(tq, dv),
        compiler_params=_cparams("arbitrary", "arbitrary"),
        name="diff_attention",
    )(lam_vecs, subln_g.reshape(1, dv), qa, qa, ka, ka, va)


def _gqa_attention(qg, kg, vg, *, q_rows, kv_rows, tq, tk):
    n_kv = kg.shape[1] // HEAD_DIM
    rep = qg.shape[1] // kg.shape[1]
    q0, nq = q_rows
    k0, nk = kv_rows
    qb, kb = q0 // tq, k0 // nk
    return pl.pallas_call(
        functools.partial(_gqa_attn_kernel, tk=tk, rep=rep),
        grid=(n_kv, nq // tq),
        in_specs=[
            pl.BlockSpec((tq, rep * HEAD_DIM), lambda g, i: (qb + i, g)),
            pl.BlockSpec((nk, HEAD_DIM), lambda g, i: (kb, g)),
            pl.BlockSpec((nk, HEAD_DIM), lambda g, i: (kb, g)),
        ],
        out_specs=pl.BlockSpec((tq, rep * HEAD_DIM), lambda g, i: (i, g)),
        out_shape=jax.ShapeDtypeStruct((nq, qg.shape[1]), BF16),
        scratch_shapes=_flash_scratch(rep * tq, HEAD_DIM),
        compiler_params=_cparams("arbitrary", "arbitrary"),
        name="gqa_attention",
    )(qg, kg, vg)


def _mla_attention(q, kv, kr, *, q_rows, kv_rows, tq, tk):
    n_heads = q.shape[1] // MLA_HEAD_PAD
    q0, nq = q_rows
    k0, nk = kv_rows
    qb, kb = q0 // tq, k0 // nk
    return pl.pallas_call(
        functools.partial(_mla_attn_kernel, tk=tk),
        grid=(n_heads, nq // tq),
        in_specs=[
            pl.BlockSpec((tq, MLA_HEAD_PAD), lambda h, i: (qb + i, h)),
            pl.BlockSpec((nk, MLA_NOPE), lambda h, i: (kb, 2 * h)),
            pl.BlockSpec((nk, LANES), lambda h, i: (kb, 0)),
            pl.BlockSpec((nk, MLA_V), lambda h, i: (kb, 2 * h + 1)),
        ],
        out_specs=pl.BlockSpec((tq, MLA_V), lambda h, i: (i, h)),
        out_shape=jax.ShapeDtypeStruct((nq, n_heads * MLA_V), BF16),
        scratch_shapes=[pltpu.VMEM((nk, MLA_HEAD_PAD), BF16)] + _flash_scratch(tq, MLA_V),
        compiler_params=_cparams("arbitrary", "arbitrary"),
        name="mla_attention",
    )(q, kv, kr, kv)


def _router_kernel(h_ref, w_ref, b_ref, g_ref):
    logits = jnp.dot(h_ref[...], w_ref[...], preferred_element_type=F32) + b_ref[...]
    lane = lax.broadcasted_iota(jnp.int32, logits.shape, 1).astype(F32)
    neg = jnp.float32(-jnp.inf)
    big = jnp.float32(LANES)

    gmask = (lane >= N_EXPERTS) & (lane < N_EXPERTS + N_GROUPS)
    gl = jnp.where(gmask, logits, neg)
    ge = jnp.exp(gl - jnp.max(gl, axis=-1, keepdims=True))
    gprob = ge / jnp.sum(ge, axis=-1, keepdims=True)
    group_w = jnp.max(gprob, axis=-1, keepdims=True)
    gidx = jnp.min(jnp.where(gmask & (gprob == group_w), lane, big), axis=-1, keepdims=True) - N_EXPERTS

    emask = (lane >= gidx * EXPERTS_PER_GROUP) & (lane < (gidx + 1) * EXPERTS_PER_GROUP)
    el = jnp.where(emask, logits, neg)
    ee = jnp.exp(el - jnp.max(el, axis=-1, keepdims=True))
    ep = ee / jnp.sum(ee, axis=-1, keepdims=True)
    p1 = jnp.max(ep, axis=-1, keepdims=True)
    i1 = jnp.min(jnp.where(emask & (ep == p1), lane, big), axis=-1, keepdims=True)
    rest = emask & (lane != i1)
    p2 = jnp.max(jnp.where(rest, ep, neg), axis=-1, keepdims=True)
    i2 = jnp.min(jnp.where(rest & (ep == p2), lane, big), axis=-1, keepdims=True)
    denom = p1 + p2
    g_ref[...] = (jnp.where(lane == i1, group_w * (p1 / denom), 0.0)
                  + jnp.where(lane == i2, group_w * (p2 / denom), 0.0))


def _router(h2, w_rt, b_rt):
    m, d = h2.shape
    return pl.pallas_call(
        _router_kernel,
        grid=(m // ROW_TILE,),
        in_specs=[pl.BlockSpec((ROW_TILE, d), lambda i: (i, 0)),
                  pl.BlockSpec((d, LANES), lambda i: (0, 0)),
                  pl.BlockSpec((1, LANES), lambda i: (0, 0))],
        out_specs=pl.BlockSpec((ROW_TILE, LANES), lambda i: (i, 0)),
        out_shape=jax.ShapeDtypeStruct((m, LANES), F32),
        compiler_params=_cparams("arbitrary"),
        name="moe_router",
    )(h2, w_rt, b_rt)


def _expert_kernel(h_ref, wg_ref, wu_ref, gates_ref, o_ref):
    e = pl.program_id(0)
    h = h_ref[...]
    g = jnp.dot(h, wg_ref[...], preferred_element_type=F32)
    u = jnp.dot(h, wu_ref[...], preferred_element_type=F32)
    gates = gates_ref[...]
    lane = lax.broadcasted_iota(jnp.int32, gates.shape, 1)
    gate = jnp.sum(jnp.where(lane == e, gates, 0.0), axis=-1, keepdims=True)
    o_ref[...] = (g * jax.nn.sigmoid(g) * u * gate).astype(o_ref.dtype)


def _experts_hidden(h2, w_gate, w_up, gates):
    m, d = h2.shape
    n_exp, _, f = w_gate.shape
    tm = MM_TM
    return pl.pallas_call(
        _expert_kernel,
        grid=(n_exp, m // tm),
        in_specs=[pl.BlockSpec((tm, d), lambda e, i: (i, 0)),
                  pl.BlockSpec((None, d, f), lambda e, i: (e, 0, 0)),
                  pl.BlockSpec((None, d, f), lambda e, i: (e, 0, 0)),
                  pl.BlockSpec((tm, LANES), lambda e, i: (i, 0))],
        out_specs=pl.BlockSpec((tm, f), lambda e, i: (i, e)),
        out_shape=jax.ShapeDtypeStruct((m, n_exp * f), BF16),
        compiler_params=_cparams("arbitrary", "arbitrary"),
        name="moe_experts",
    )(h2, w_gate, w_up, gates)


def kernel(x, c, ctx, c_ctx, w_ada, b_ada, ln1_g, ln1_b, ln2_g, ln2_b, ev_w_in, ev_w_out, diff_lambda, diff_subln_g, gqa_q_norm_g, gqa_k_norm_g, od_w_in, mla_q_norm_g, mla_kv_norm_g, mla_w_uq, mla_w_ukv, od_w_out, moe_w_group, moe_b_group, moe_w_router, moe_b_router, moe_w_gate, moe_w_up, moe_w_down):
    batch, seq, d = x.shape
    n_ctx = ctx.shape[1]
    depth = w_ada.shape[0]
    assert batch == 1 and seq % 1024 == 0 and n_ctx == ROW_TILE and (seq + n_ctx) % MM_TM == 0
    m = seq + n_ctx
    alpha = (2.0 * depth) ** 0.25
    lat_rows, ctx_rows = (0, seq), (seq, n_ctx)
    all_rows = (0, m)

    xs = jnp.concatenate([x.reshape(seq, d), ctx.reshape(n_ctx, d)], axis=0)
    mods3 = _ada_mods(c, c_ctx, w_ada, b_ada).reshape(depth * 2 * N_MOD, 1, d)
    tab_a = _rope_tables(seq, n_ctx, HEAD_DIM)
    tab_m = _rope_tables(seq, n_ctx, MLA_ROPE)

    h = _modulate(xs, mods3, 0, 1, 0, seq)
    for layer in range(depth):
        i = layer // 2
        if layer % 2 == 0:
            lambda_init = 0.8 - 0.6 * math.exp(-0.3 * layer)
            scale = 1.0 / math.sqrt(HEAD_DIM)
            w_in = ev_w_in[i]
            n_diff_qk = d // 2
            n_diff_v = d // 2
            n_gqa_q = d // 2
            n_gqa_kv = n_gqa_q // 4
            o0 = 0
            segs = []
            for width in (n_diff_qk, n_diff_qk, n_diff_v, n_gqa_q, n_gqa_kv, n_gqa_kv):
                segs.append(w_in[:, o0:o0 + width].astype(BF16))
                o0 += width
            qa = _head_proj(h, segs[0], tn=512, tables=tab_a, scale=scale, name="proj_diff_q")
            ka = _head_proj(h, segs[1], tn=512, tables=tab_a, name="proj_diff_k")
            va = _matmul(h, segs[2], BF16, tn=512, name="proj_diff_v")
            qg = _head_proj(h, segs[3], tn=512, norm_g=gqa_q_norm_g[i], tables=tab_a, scale=scale,
                            name="proj_gqa_q")
            kg = _head_proj(h, segs[4], tn=512, norm_g=gqa_k_norm_g[i], tables=tab_a, name="proj_gqa_k")
            vg = _matmul(h, segs[5], BF16, tn=512, name="proj_gqa_v")
            parts = []
            for q_rows, kv_rows, tq_d, tq_g, tk in ((lat_rows, all_rows, 1024, 256, KV_CHUNK),
                                                    (ctx_rows, ctx_rows, n_ctx, n_ctx, n_ctx)):
                od = _diff_attention(qa, ka, va, diff_lambda[i], diff_subln_g[i], lambda_init,
                                     q_rows=q_rows, kv_rows=kv_rows, tq=tq_d, tk=tk)
                og = _gqa_attention(qg, kg, vg, q_rows=q_rows, kv_rows=kv_rows, tq=tq_g, tk=tk)
                parts.append(jnp.concatenate([od, og], axis=1))
            attn = jnp.concatenate(parts, axis=0)
            w_out = ev_w_out[i].astype(BF16)
        else:
            scale = 1.0 / math.sqrt(MLA_NOPE + MLA_ROPE)
            w_in = od_w_in[i]
            q_rank = mla_q_norm_g.shape[1]
            kv_rank = mla_kv_norm_g.shape[1]
            n_heads = mla_w_uq.shape[2] // (MLA_NOPE + MLA_ROPE)
            vec = lambda n: pl.BlockSpec((1, n), lambda j, i_: (0, 0))
            tab_specs = [pl.BlockSpec((MM_TM, LANES), lambda j, i_: (i_, 0))] * 3
            cq = _matmul(h, w_in[:, :q_rank].astype(BF16), BF16, tn=q_rank, epilogue=_rmsnorm_epilogue,
                         extras=[mla_q_norm_g[i].reshape(1, q_rank)], extra_specs=[vec(q_rank)],
                         name="proj_mla_cq")
            ckv = _matmul(h, w_in[:, q_rank:q_rank + kv_rank].astype(BF16), BF16, tn=kv_rank,
                          epilogue=_rmsnorm_epilogue, extras=[mla_kv_norm_g[i].reshape(1, kv_rank)],
                          extra_specs=[vec(kv_rank)], name="proj_mla_ckv")
            w_kr = jnp.pad(w_in[:, q_rank + kv_rank:], ((0, 0), (0, LANES - MLA_ROPE))).astype(BF16)
            kr = _matmul(h, w_kr, BF16, tn=LANES, epilogue=_rope_m_epilogue, extras=tab_m,
                         extra_specs=tab_specs, name="proj_mla_krope")
            w_uq = jnp.pad(mla_w_uq[i].reshape(q_rank, n_heads, MLA_NOPE + MLA_ROPE),
                           ((0, 0), (0, 0), (0, MLA_HEAD_PAD - MLA_NOPE - MLA_ROPE)))
            w_uq = w_uq.reshape(q_rank, n_heads * MLA_HEAD_PAD).astype(BF16)
            q = _matmul(cq, w_uq, BF16, tn=1024, epilogue=functools.partial(_mla_q_epilogue, scale=scale),
                        extras=tab_m, extra_specs=tab_specs, name="proj_mla_q")
            kv = _matmul(ckv, mla_w_ukv[i].astype(BF16), BF16, tn=1024, name="proj_mla_kv")
            attn = jnp.concatenate([
                _mla_attention(q, kv, kr, q_rows=lat_rows, kv_rows=all_rows, tq=1024, tk=KV_CHUNK),
                _mla_attention(q, kv, kr, q_rows=ctx_rows, kv_rows=ctx_rows, tq=n_ctx, tk=n_ctx)], axis=0)
            w_out = od_w_out[i].astype(BF16)

        o = _matmul(attn, w_out, F32, tn=1024, name="proj_out")
        xs, h2 = _resid_ln(xs, o, mods3, layer, 2, ln1_g[layer], ln1_b[layer], layer, 4, 3, seq, alpha)

        w_rt = jnp.concatenate([moe_w_router[layer], moe_w_group[layer],
                                jnp.zeros((d, LANES - N_EXPERTS - N_GROUPS), F32)], axis=1).astype(BF16)
        b_rt = jnp.concatenate([moe_b_router[layer], moe_b_group[layer],
                                jnp.zeros((LANES - N_EXPERTS - N_GROUPS,), F32)]).reshape(1, LANES)
        gates = _router(h2, w_rt, b_rt)
        hidden = _experts_hidden(h2, moe_w_gate[layer].astype(BF16), moe_w_up[layer].astype(BF16), gates)
        n_exp, f, _ = moe_w_down[layer].shape
        y = _matmul(hidden, moe_w_down[layer].reshape(n_exp * f, d).astype(BF16), F32, tn=512,
                    name="moe_down")
        nxt = min(layer + 1, depth - 1)
        xs, h = _resid_ln(xs, y, mods3, layer, 5, ln2_g[layer], ln2_b[layer], nxt, 1, 0, seq, alpha)
    return xs[:seq].reshape(batch, seq, d)
```

```python
import functools
import math

import jax
import jax.numpy as jnp
from jax import lax
from jax.experimental import pallas as pl
from jax.experimental.pallas import tpu as pltpu

F32 = jnp.float32
BF16 = jnp.bfloat16

GRID_W = 64
HEAD_DIM = 128
ROPE_THETA = 10000.0
LN_EPS = 1e-5
RMS_EPS = 1e-6
N_GROUPS = 4
EXPERTS_PER_GROUP = 4
N_EXPERTS = N_GROUPS * EXPERTS_PER_GROUP
N_MOD = 6
MLA_NOPE = 128
MLA_ROPE = 64
MLA_V = 128
MLA_HEAD_PAD = 256
LOG2E = math.log2(math.e)

LANES = 128
V7X_VMEM_LIMIT = 56 * 2**20

ROW_TILE = 256
MM_TM = 768
KV_CHUNK = 768


def _cparams(*sem):
    return pltpu.CompilerParams(dimension_semantics=sem, vmem_limit_bytes=V7X_VMEM_LIMIT)


def _ada_kernel(ct_ref, w_ref, b_ref, o_ref):
    ct = ct_ref[...]
    s = ct * jax.nn.sigmoid(ct)
    w = w_ref[...]
    b = b_ref[...]
    for r in range(2):
        o_ref[r:r + 1, :] = jnp.sum(w * s[:, r:r + 1], axis=0, keepdims=True) + b


def _ada_mods(c, c_ctx, w_ada, b_ada):
    depth, d, n = w_ada.shape
    tn = 512
    ct = jnp.stack([c.reshape(d), c_ctx.reshape(d)], axis=1)
    return pl.pallas_call(
        _ada_kernel,
        grid=(depth, n // tn),
        in_specs=[
            pl.BlockSpec((d, 2), lambda l, j: (0, 0)),
            pl.BlockSpec((None, d, tn), lambda l, j: (l, 0, j)),
            pl.BlockSpec((None, 1, tn), lambda l, j: (l, 0, j)),
        ],
        out_specs=pl.BlockSpec((None, 2, tn), lambda l, j: (l, 0, j)),
        out_shape=jax.ShapeDtypeStruct((depth, 2, n), F32),
        compiler_params=_cparams("arbitrary", "arbitrary"),
        name="ada_mods",
    )(ct, w_ada, b_ada.reshape(depth, 1, n))


def _mod_spec(layer, which, d, seq):
    def index(i):
        sel = jnp.where(i * ROW_TILE >= seq, 1, 0)
        return ((layer * 2 + sel) * N_MOD + which, 0, 0)
    return pl.BlockSpec((None, 1, d), index)


def _modulate_kernel(x_ref, sc_ref, sh_ref, h_ref):
    h_ref[...] = (x_ref[...] * (1.0 + sc_ref[...]) + sh_ref[...]).astype(h_ref.dtype)


def _modulate(x, mods3, layer, sc_idx, sh_idx, seq):
    m, d = x.shape
    return pl.pallas_call(
        _modulate_kernel,
        grid=(m // ROW_TILE,),
        in_specs=[pl.BlockSpec((ROW_TILE, d), lambda i: (i, 0)),
                  _mod_spec(layer, sc_idx, d, seq), _mod_spec(layer, sh_idx, d, seq)],
        out_specs=pl.BlockSpec((ROW_TILE, d), lambda i: (i, 0)),
        out_shape=jax.ShapeDtypeStruct((m, d), BF16),
        compiler_params=_cparams("arbitrary"),
        name="modulate",
    )(x, mods3, mods3)


def _resid_ln_kernel(x_ref, o_ref, g_ref, lg_ref, lb_ref, sc_ref, sh_ref, y_ref, h_ref, *, alpha):
    z = alpha * x_ref[...] + g_ref[...] * o_ref[...]
    mu = jnp.mean(z, axis=-1, keepdims=True)
    zc = z - mu
    var = jnp.mean(zc * zc, axis=-1, keepdims=True)
    y = zc * lax.rsqrt(var + LN_EPS) * lg_ref[...] + lb_ref[...]
    y_ref[...] = y
    h_ref[...] = (y * (1.0 + sc_ref[...]) + sh_ref[...]).astype(h_ref.dtype)


def _resid_ln(x, o, mods3, layer, gate_idx, ln_g, ln_b, nxt_layer, sc_idx, sh_idx, seq, alpha):
    m, d = x.shape
    row = pl.BlockSpec((ROW_TILE, d), lambda i: (i, 0))
    vec = pl.BlockSpec((1, d), lambda i: (0, 0))
    return pl.pallas_call(
        functools.partial(_resid_ln_kernel, alpha=alpha),
        grid=(m // ROW_TILE,),
        in_specs=[row, row, _mod_spec(layer, gate_idx, d, seq), vec, vec,
                  _mod_spec(nxt_layer, sc_idx, d, seq), _mod_spec(nxt_layer, sh_idx, d, seq)],
        out_specs=[row, row],
        out_shape=[jax.ShapeDtypeStruct((m, d), F32), jax.ShapeDtypeStruct((m, d), BF16)],
        compiler_params=_cparams("arbitrary"),
        name="resid_ln",
    )(x, o, mods3, ln_g.reshape(1, d), ln_b.reshape(1, d), mods3, mods3)


def _mm_kernel(*refs, epilogue, n_extra):
    a_ref, b_ref = refs[0], refs[1]
    extras = refs[2:2 + n_extra]
    o_ref = refs[2 + n_extra]
    acc = jnp.dot(a_ref[...], b_ref[...], preferred_element_type=F32)
    epilogue(acc, o_ref, *extras)


def _store(acc, o_ref):
    o_ref[...] = acc.astype(o_ref.dtype)


def _matmul(a, b, out_dtype, *, tn, epilogue=_store, extras=(), extra_specs=(), name="matmul"):
    m, k = a.shape
    n = b.shape[1]
    tm = MM_TM
    tn = min(tn, n)
    assert m % tm == 0 and n % tn == 0
    return pl.pallas_call(
        functools.partial(_mm_kernel, epilogue=epilogue, n_extra=len(extras)),
        grid=(n // tn, m // tm),
        in_specs=[pl.BlockSpec((tm, k), lambda j, i: (i, 0)),
                  pl.BlockSpec((k, tn), lambda j, i: (0, j)),
                  *extra_specs],
        out_specs=pl.BlockSpec((tm, tn), lambda j, i: (i, j)),
        out_shape=jax.ShapeDtypeStruct((m, n), out_dtype),
        compiler_params=_cparams("arbitrary", "arbitrary"),
        name=name,
    )(a, b, *extras)


def _rope_tables(seq, n_ctx, rot_dim):
    rows = seq // GRID_W
    row, col = jnp.meshgrid(jnp.arange(rows), jnp.arange(GRID_W), indexing="ij")
    row = row.reshape(-1).astype(F32)
    col = col.reshape(-1).astype(F32)
    quarter = rot_dim // 4
    inv_freq = ROPE_THETA ** (-jnp.arange(quarter, dtype=F32) / quarter)
    ang = jnp.concatenate([row[:, None] * inv_freq, col[:, None] * inv_freq], axis=-1)
    cos, sin = jnp.cos(ang), jnp.sin(ang)
    half = rot_dim // 2
    pad = LANES - rot_dim
    zeros_h = jnp.zeros((seq, half), F32)
    zeros_p = jnp.zeros((seq, pad), F32)
    cos_t = jnp.concatenate([cos, cos, zeros_p], axis=-1)
    sin_lo = jnp.concatenate([-sin, zeros_h, zeros_p], axis=-1)
    sin_hi = jnp.concatenate([zeros_h, sin, zeros_p], axis=-1)
    ident = jnp.concatenate([jnp.ones((n_ctx, rot_dim), F32), jnp.zeros((n_ctx, pad), F32)], axis=-1)
    zc = jnp.zeros((n_ctx, LANES), F32)
    return (jnp.concatenate([cos_t, ident], axis=0), jnp.concatenate([sin_lo, zc], axis=0),
            jnp.concatenate([sin_hi, zc], axis=0))


def _rope_block(x, cos, sin_lo, sin_hi, half):
    return (x * cos + pltpu.roll(x, LANES - half, axis=1) * sin_lo
            + pltpu.roll(x, half, axis=1) * sin_hi)


def _head_epilogue(acc, o_ref, *extras, norm, rope, scale):
    idx = 0
    if norm:
        g = extras[idx][...]
        idx += 1
    if rope:
        cos, sin_lo, sin_hi = (r[...] for r in extras[idx:idx + 3])
    for hb in range(acc.shape[1] // LANES):
        x = acc[:, hb * LANES:(hb + 1) * LANES]
        if norm:
            x = x * lax.rsqrt(jnp.mean(x * x, axis=-1, keepdims=True) + RMS_EPS) * g
        if rope:
            x = _rope_block(x, cos, sin_lo, sin_hi, HEAD_DIM // 2)
        if scale != 1.0:
            x = x * scale
        o_ref[:, hb * LANES:(hb + 1) * LANES] = x.astype(o_ref.dtype)


def _head_proj(h, w, *, tn, norm_g=None, tables=None, scale=1.0, name):
    extras, specs = [], []
    if norm_g is not None:
        extras.append(norm_g.reshape(1, LANES))
        specs.append(pl.BlockSpec((1, LANES), lambda j, i: (0, 0)))
    if tables is not None:
        extras.extend(tables)
        specs.extend([pl.BlockSpec((MM_TM, LANES), lambda j, i: (i, 0))] * 3)
    ep = functools.partial(_head_epilogue, norm=norm_g is not None, rope=tables is not None, scale=scale)
    return _matmul(h, w, BF16, tn=tn, epilogue=ep, extras=extras, extra_specs=specs, name=name)


def _rmsnorm_epilogue(acc, o_ref, g_ref):
    x = acc * lax.rsqrt(jnp.mean(acc * acc, axis=-1, keepdims=True) + RMS_EPS) * g_ref[...]
    o_ref[...] = x.astype(o_ref.dtype)


def _rope_m_epilogue(acc, o_ref, cos_ref, lo_ref, hi_ref):
    o_ref[...] = _rope_block(acc, cos_ref[...], lo_ref[...], hi_ref[...], MLA_ROPE // 2).astype(o_ref.dtype)


def _mla_q_epilogue(acc, o_ref, cos_ref, lo_ref, hi_ref, *, scale):
    cos, lo, hi = cos_ref[...], lo_ref[...], hi_ref[...]
    for hd in range(acc.shape[1] // MLA_HEAD_PAD):
        base = hd * MLA_HEAD_PAD
        o_ref[:, base:base + LANES] = (acc[:, base:base + LANES] * scale).astype(o_ref.dtype)
        r = _rope_block(acc[:, base + LANES:base + 2 * LANES], cos, lo, hi, MLA_ROPE // 2)
        o_ref[:, base + LANES:base + 2 * LANES] = (r * scale).astype(o_ref.dtype)


def _softmax_step(s, v, m_ref, l_ref, acc_ref):
    m_old = m_ref[...]
    m_new = jnp.maximum(m_old, jnp.max(s, axis=-1, keepdims=True))
    p = jnp.concatenate([jnp.exp2(s[:, j * LANES:(j + 1) * LANES] - m_new)
                         for j in range(s.shape[1] // LANES)], axis=1)
    alpha = jnp.exp2(m_old - m_new)
    if l_ref is not None:
        l_ref[...] = alpha * l_ref[...] + jnp.sum(p, axis=-1, keepdims=True)
    pv = jnp.dot(p.astype(BF16), v, preferred_element_type=F32)
    acc_ref[...] = jnp.concatenate([alpha] * (acc_ref.shape[1] // LANES), axis=1) * acc_ref[...] + pv
    m_ref[...] = m_new


def _flash(q, k_ref, v_ref, m_ref, l_ref, acc_ref, s_ref, tk):
    n_chunks = k_ref.shape[0] // tk
    m_ref[...] = jnp.full(m_ref.shape, -jnp.inf, F32)
    acc_ref[...] = jnp.zeros(acc_ref.shape, F32)
    if l_ref is not None:
        l_ref[...] = jnp.zeros(l_ref.shape, F32)
    nt_dims = (((1,), (1,)), ((), ()))
    s_ref[...] = lax.dot_general(q, k_ref[pl.ds(0, tk), :], nt_dims, preferred_element_type=F32)

    def body(c, carry):
        start = pl.multiple_of(c * tk, tk)
        nxt = pl.multiple_of((c + 1) * tk, tk)
        s = s_ref[...]
        s_next = lax.dot_general(q, k_ref[pl.ds(nxt, tk), :], nt_dims, preferred_element_type=F32)
        _softmax_step(s, v_ref[pl.ds(start, tk), :], m_ref, l_ref, acc_ref)
        s_ref[...] = s_next
        return carry

    lax.fori_loop(0, n_chunks - 1, body, 0)
    _softmax_step(s_ref[...], v_ref[pl.ds((n_chunks - 1) * tk, tk), :], m_ref, l_ref, acc_ref)


def _ones_behind(v_ref, vaug_ref):
    vaug_ref[:, :LANES] = v_ref[...]
    vaug_ref[:, LANES:] = jnp.ones((v_ref.shape[0], LANES), BF16)


def _diff_attn_kernel(lam_ref, g_ref, q1_ref, q2_ref, k1_ref, k2_ref, v_ref, o_ref,
                      m_ref, l_ref, acc_ref, s_ref, a1_ref, *, tk, lambda_init):
    lv = lam_ref[...]
    lam = (jnp.exp(jnp.sum(lv[0:1] * lv[1:2], axis=-1, keepdims=True))
           - jnp.exp(jnp.sum(lv[2:3] * lv[3:4], axis=-1, keepdims=True)) + lambda_init)
    n_rep = acc_ref.shape[1] // LANES
    _flash(q1_ref[...], k1_ref, v_ref, m_ref, l_ref, acc_ref, s_ref, tk)
    a1_ref[...] = acc_ref[...] / jnp.concatenate([l_ref[...]] * n_rep, axis=1)
    _flash(q2_ref[...], k2_ref, v_ref, m_ref, l_ref, acc_ref, s_ref, tk)
    o = a1_ref[...] - lam * (acc_ref[...] / jnp.concatenate([l_ref[...]] * n_rep, axis=1))
    o = o * lax.rsqrt(jnp.mean(o * o, axis=-1, keepdims=True) + RMS_EPS) * g_ref[...]
    o_ref[...] = (o * (1.0 - lambda_init)).astype(o_ref.dtype)


def _gqa_attn_kernel(q_ref, k_ref, v_ref, o_ref, vaug_ref, qs_ref, m_ref, acc_ref, s_ref, *, tk, rep):
    tq = q_ref.shape[0]

    @pl.when(pl.program_id(1) == 0)
    def _():
        _ones_behind(v_ref, vaug_ref)

    for r in range(rep):
        qs_ref[r * tq:(r + 1) * tq, :] = q_ref[:, r * LANES:(r + 1) * LANES]
    _flash(qs_ref[...], k_ref, vaug_ref, m_ref, None, acc_ref, s_ref, tk)
    for r in range(rep):
        rows = slice(r * tq, (r + 1) * tq)
        o_ref[:, r * LANES:(r + 1) * LANES] = (acc_ref[rows, :LANES] / acc_ref[rows, LANES:]).astype(o_ref.dtype)


def _mla_attn_kernel(q_ref, kn_ref, kr_ref, v_ref, o_ref, kcat_ref, vaug_ref, m_ref, acc_ref, s_ref, *, tk):
    @pl.when(pl.program_id(1) == 0)
    def _():
        kcat_ref[:, :LANES] = kn_ref[...]
        kcat_ref[:, LANES:] = kr_ref[...]
        _ones_behind(v_ref, vaug_ref)

    _flash(q_ref[...], kcat_ref, vaug_ref, m_ref, None, acc_ref, s_ref, tk)
    o_ref[...] = (acc_ref[:, :LANES] / acc_ref[:, LANES:]).astype(o_ref.dtype)


def _diff_attention(qa, ka, va, lam_vecs, subln_g, lambda_init, *, q_rows, kv_rows, tq, tk):
    n_heads = va.shape[1] // (2 * HEAD_DIM)
    q0, nq = q_rows
    k0, nk = kv_rows
    qb, kb = q0 // tq, k0 // nk
    dv = 2 * HEAD_DIM
    return pl.pallas_call(
        functools.partial(_diff_attn_kernel, tk=tk, lambda_init=lambda_init),
        grid=(n_heads, nq // tq),
        in_specs=[
            pl.BlockSpec((4, HEAD_DIM), lambda h, i: (0, 0)),
            pl.BlockSpec((1, dv), lambda h, i: (0, 0)),
            pl.BlockSpec((tq, HEAD_DIM), lambda h, i: (qb + i, 2 * h)),
            pl.BlockSpec((tq, HEAD_DIM), lambda h, i: (qb + i, 2 * h + 1)),
            pl.BlockSpec((nk, HEAD_DIM), lambda h, i: (kb, 2 * h)),
            pl.BlockSpec((nk, HEAD_DIM), lambda h, i: (kb, 2 * h + 1)),
            pl.BlockSpec((nk, dv), lambda h, i: (kb, h)),
        ],
        out_specs=pl.BlockSpec((tq, dv), lambda h, i: (i, h)),
        out_shape=jax.ShapeDtypeStruct((nq, n_heads * dv), BF16),
        scratch_shapes=[pltpu.VMEM((tq, LANES), F32), pltpu.VMEM((tq, LANES), F32), pltpu.VMEM((tq, dv), F32),
                        pltpu.VMEM((tq, tk), F32), pltpu.VMEM((tq, dv), F32)],
        compiler_params=_cparams("arbitrary", "arbitrary"),
        name="diff_attention",
    )(lam_vecs, subln_g.reshape(1, dv), qa, qa, ka, ka, va)


def _gqa_attention(qg, kg, vg, *, q_rows, kv_rows, tq, tk):
    n_kv = kg.shape[1] // HEAD_DIM
    rep = qg.shape[1] // kg.shape[1]
    q0, nq = q_rows
    k0, nk = kv_rows
    qb, kb = q0 // tq, k0 // nk
    rows = rep * tq
    return pl.pallas_call(
        functools.partial(_gqa_attn_kernel, tk=tk, rep=rep),
        grid=(n_kv, nq // tq),
        in_specs=[
            pl.BlockSpec((tq, rep * HEAD_DIM), lambda g, i: (qb + i, g)),
            pl.BlockSpec((nk, HEAD_DIM), lambda g, i: (kb, g)),
            pl.BlockSpec((nk, HEAD_DIM), lambda g, i: (kb, g)),
        ],
        out_specs=pl.BlockSpec((tq, rep * HEAD_DIM), lambda g, i: (i, g)),
        out_shape=jax.ShapeDtypeStruct((nq, qg.shape[1]), BF16),
        scratch_shapes=[pltpu.VMEM((nk, 2 * LANES), BF16), pltpu.VMEM((rows, HEAD_DIM), BF16),
                        pltpu.VMEM((rows, LANES), F32), pltpu.VMEM((rows, 2 * LANES), F32),
                        pltpu.VMEM((rows, tk), F32)],
        compiler_params=_cparams("arbitrary", "arbitrary"),
        name="gqa_attention",
    )(qg, kg, vg)


def _mla_attention(q, kv, kr, *, q_rows, kv_rows, tq, tk):
    n_heads = q.shape[1] // MLA_HEAD_PAD
    q0, nq = q_rows
    k0, nk = kv_rows
    qb, kb = q0 // tq, k0 // nk
    return pl.pallas_call(
        functools.partial(_mla_attn_kernel, tk=tk),
        grid=(n_heads, nq // tq),
        in_specs=[
            pl.BlockSpec((tq, MLA_HEAD_PAD), lambda h, i: (qb + i, h)),
            pl.BlockSpec((nk, MLA_NOPE), lambda h, i: (kb, 2 * h)),
            pl.BlockSpec((nk, LANES), lambda h, i: (kb, 0)),
            pl.BlockSpec((nk, MLA_V), lambda h, i: (kb, 2 * h + 1)),
        ],
        out_specs=pl.BlockSpec((tq, MLA_V), lambda h, i: (i, h)),
        out_shape=jax.ShapeDtypeStruct((nq, n_heads * MLA_V), BF16),
        scratch_shapes=[pltpu.VMEM((nk, MLA_HEAD_PAD), BF16), pltpu.VMEM((nk, 2 * LANES), BF16),
                        pltpu.VMEM((tq, LANES), F32), pltpu.VMEM((tq, 2 * LANES), F32),
                        pltpu.VMEM((tq, tk), F32)],
        compiler_params=_cparams("arbitrary", "arbitrary"),
        name="mla_attention",
    )(q, kv, kr, kv)


def _router_kernel(h_ref, w_ref, b_ref, g_ref):
    logits = jnp.dot(h_ref[...], w_ref[...], preferred_element_type=F32) + b_ref[...]
    lane = lax.broadcasted_iota(jnp.int32, logits.shape, 1).astype(F32)
    neg = jnp.float32(-jnp.inf)
    big = jnp.float32(LANES)

    gmask = (lane >= N_EXPERTS) & (lane < N_EXPERTS + N_GROUPS)
    gl = jnp.where(gmask, logits, neg)
    ge = jnp.exp(gl - jnp.max(gl, axis=-1, keepdims=True))
    gprob = ge / jnp.sum(ge, axis=-1, keepdims=True)
    group_w = jnp.max(gprob, axis=-1, keepdims=True)
    gidx = jnp.min(jnp.where(gmask & (gprob == group_w), lane, big), axis=-1, keepdims=True) - N_EXPERTS

    emask = (lane >= gidx * EXPERTS_PER_GROUP) & (lane < (gidx + 1) * EXPERTS_PER_GROUP)
    el = jnp.where(emask, logits, neg)
    ee = jnp.exp(el - jnp.max(el, axis=-1, keepdims=True))
    ep = ee / jnp.sum(ee, axis=-1, keepdims=True)
    p1 = jnp.max(ep, axis=-1, keepdims=True)
    i1 = jnp.min(jnp.where(emask & (ep == p1), lane, big), axis=-1, keepdims=True)
    rest = emask & (lane != i1)
    p2 = jnp.max(jnp.where(rest, ep, neg), axis=-1, keepdims=True)
    i2 = jnp.min(jnp.where(rest & (ep == p2), lane, big), axis=-1, keepdims=True)
    denom = p1 + p2
    g_ref[...] = (jnp.where(lane == i1, group_w * (p1 / denom), 0.0)
                  + jnp.where(lane == i2, group_w * (p2 / denom), 0.0))


def _router(h2, w_rt, b_rt):
    m, d = h2.shape
    return pl.pallas_call(
        _router_kernel,
        grid=(m // ROW_TILE,),
        in_specs=[pl.BlockSpec((ROW_TILE, d), lambda i: (i, 0)),
                  pl.BlockSpec((d, LANES), lambda i: (0, 0)),
                  pl.BlockSpec((1, LANES), lambda i: (0, 0))],
        out_specs=pl.BlockSpec((ROW_TILE, LANES), lambda i: (i, 0)),
        out_shape=jax.ShapeDtypeStruct((m, LANES), F32),
        compiler_params=_cparams("arbitrary"),
        name="moe_router",
    )(h2, w_rt, b_rt)


def _expert_kernel(h_ref, wg_ref, wu_ref, gates_ref, o_ref):
    e = pl.program_id(0)
    h = h_ref[...]
    g = jnp.dot(h, wg_ref[...], preferred_element_type=F32)
    u = jnp.dot(h, wu_ref[...], preferred_element_type=F32)
    gates = gates_ref[...]
    lane = lax.broadcasted_iota(jnp.int32, gates.shape, 1)
    gate = jnp.sum(jnp.where(lane == e, gates, 0.0), axis=-1, keepdims=True)
    o_ref[...] = (g * jax.nn.sigmoid(g) * u * gate).astype(o_ref.dtype)


def _experts_hidden(h2, w_gate, w_up, gates):
    m, d = h2.shape
    n_exp, _, f = w_gate.shape
    tm = MM_TM
    return pl.pallas_call(
        _expert_kernel,
        grid=(n_exp, m // tm),
        in_specs=[pl.BlockSpec((tm, d), lambda e, i: (i, 0)),
                  pl.BlockSpec((None, d, f), lambda e, i: (e, 0, 0)),
                  pl.BlockSpec((None, d, f), lambda e, i: (e, 0, 0)),
                  pl.BlockSpec((tm, LANES), lambda e, i: (i, 0))],
        out_specs=pl.BlockSpec((tm, f), lambda e, i: (i, e)),
        out_shape=jax.ShapeDtypeStruct((m, n_exp * f), BF16),
        compiler_params=_cparams("arbitrary", "arbitrary"),
        name="moe_experts",
    )(h2, w_gate, w_up, gates)


def kernel(x, c, ctx, c_ctx, w_ada, b_ada, ln1_g, ln1_b, ln2_g, ln2_b, ev_w_in, ev_w_out, diff_lambda, diff_subln_g, gqa_q_norm_g, gqa_k_norm_g, od_w_in, mla_q_norm_g, mla_kv_norm_g, mla_w_uq, mla_w_ukv, od_w_out, moe_w_group, moe_b_group, moe_w_router, moe_b_router, moe_w_gate, moe_w_up, moe_w_down):
    batch, seq, d = x.shape
    n_ctx = ctx.shape[1]
    depth = w_ada.shape[0]
    assert batch == 1 and seq % 1024 == 0 and n_ctx == ROW_TILE and (seq + n_ctx) % MM_TM == 0
    m = seq + n_ctx
    alpha = (2.0 * depth) ** 0.25
    lat_rows, ctx_rows = (0, seq), (seq, n_ctx)
    all_rows = (0, m)

    xs = jnp.concatenate([x.reshape(seq, d), ctx.reshape(n_ctx, d)], axis=0)
    mods3 = _ada_mods(c, c_ctx, w_ada, b_ada).reshape(depth * 2 * N_MOD, 1, d)
    tab_a = _rope_tables(seq, n_ctx, HEAD_DIM)
    tab_m = _rope_tables(seq, n_ctx, MLA_ROPE)

    h = _modulate(xs, mods3, 0, 1, 0, seq)
    for layer in range(depth):
        i = layer // 2
        if layer % 2 == 0:
            lambda_init = 0.8 - 0.6 * math.exp(-0.3 * layer)
            scale = LOG2E / math.sqrt(HEAD_DIM)
            w_in = ev_w_in[i]
            n_diff_qk = d // 2
            n_diff_v = d // 2
            n_gqa_q = d // 2
            n_gqa_kv = n_gqa_q // 4
            o0 = 0
            segs = []
            for width in (n_diff_qk, n_diff_qk, n_diff_v, n_gqa_q, n_gqa_kv, n_gqa_kv):
                segs.append(w_in[:, o0:o0 + width].astype(BF16))
                o0 += width
            qa = _head_proj(h, segs[0], tn=512, tables=tab_a, scale=scale, name="proj_diff_q")
            ka = _head_proj(h, segs[1], tn=512, tables=tab_a, name="proj_diff_k")
            va = _matmul(h, segs[2], BF16, tn=512, name="proj_diff_v")
            qg = _head_proj(h, segs[3], tn=512, norm_g=gqa_q_norm_g[i], tables=tab_a, scale=scale,
                            name="proj_gqa_q")
            kg = _head_proj(h, segs[4], tn=512, norm_g=gqa_k_norm_g[i], tables=tab_a, name="proj_gqa_k")
            vg = _matmul(h, segs[5], BF16, tn=512, name="proj_gqa_v")
            parts = []
            for q_rows, kv_rows, tq_d, tq_g, tk in ((lat_rows, all_rows, 1024, 256, KV_CHUNK),
                                                    (ctx_rows, ctx_rows, n_ctx, n_ctx, n_ctx)):
                od = _diff_attention(qa, ka, va, diff_lambda[i], diff_subln_g[i], lambda_init,
                                     q_rows=q_rows, kv_rows=kv_rows, tq=tq_d, tk=tk)
                og = _gqa_attention(qg, kg, vg, q_rows=q_rows, kv_rows=kv_rows, tq=tq_g, tk=tk)
                parts.append(jnp.concatenate([od, og], axis=1))
            attn = jnp.concatenate(parts, axis=0)
            w_out = ev_w_out[i].astype(BF16)
        else:
            scale = LOG2E / math.sqrt(MLA_NOPE + MLA_ROPE)
            w_in = od_w_in[i]
            q_rank = mla_q_norm_g.shape[1]
            kv_rank = mla_kv_norm_g.shape[1]
            n_heads = mla_w_uq.shape[2] // (MLA_NOPE + MLA_ROPE)
            vec = lambda n: pl.BlockSpec((1, n), lambda j, i_: (0, 0))
            tab_specs = [pl.BlockSpec((MM_TM, LANES), lambda j, i_: (i_, 0))] * 3
            cq = _matmul(h, w_in[:, :q_rank].astype(BF16), BF16, tn=q_rank, epilogue=_rmsnorm_epilogue,
                         extras=[mla_q_norm_g[i].reshape(1, q_rank)], extra_specs=[vec(q_rank)],
                         name="proj_mla_cq")
            ckv = _matmul(h, w_in[:, q_rank:q_rank + kv_rank].astype(BF16), BF16, tn=kv_rank,
                          epilogue=_rmsnorm_epilogue, extras=[mla_kv_norm_g[i].reshape(1, kv_rank)],
                          extra_specs=[vec(kv_rank)], name="proj_mla_ckv")
            w_kr = jnp.pad(w_in[:, q_rank + kv_rank:], ((0, 0), (0, LANES - MLA_ROPE))).astype(BF16)
            kr = _matmul(h, w_kr, BF16, tn=LANES, epilogue=_rope_m_epilogue, extras=tab_m,
                         extra_specs=tab_specs, name="proj_mla_krope")
            w_uq = jnp.pad(mla_w_uq[i].reshape(q_rank, n_heads, MLA_NOPE + MLA_ROPE),
                           ((0, 0), (0, 0), (0, MLA_HEAD_PAD - MLA_NOPE - MLA_ROPE)))
            w_uq = w_uq.reshape(q_rank, n_heads * MLA_HEAD_PAD).astype(BF16)
            q = _matmul(cq, w_uq, BF16, tn=1024, epilogue=functools.partial(_mla_q_epilogue, scale=scale),
                        extras=tab_m, extra_specs=tab_specs, name="proj_mla_q")
            kv = _matmul(ckv, mla_w_ukv[i].astype(BF16), BF16, tn=1024, name="proj_mla_kv")
            attn = jnp.concatenate([
                _mla_attention(q, kv, kr, q_rows=lat_rows, kv_rows=all_rows, tq=1024, tk=KV_CHUNK),
                _mla_attention(q, kv, kr, q_rows=ctx_rows, kv_rows=ctx_rows, tq=n_ctx, tk=n_ctx)], axis=0)
            w_out = od_w_out[i].astype(BF16)

        o = _matmul(attn, w_out, F32, tn=1024, name="proj_out")
        xs, h2 = _resid_ln(xs, o, mods3, layer, 2, ln1_g[layer], ln1_b[layer], layer, 4, 3, seq, alpha)

        w_rt = jnp.concatenate([moe_w_router[layer], moe_w_group[layer],
                                jnp.zeros((d, LANES - N_EXPERTS - N_GROUPS), F32)], axis=1).astype(BF16)
        b_rt = jnp.concatenate([moe_b_router[layer], moe_b_group[layer],
                                jnp.zeros((LANES - N_EXPERTS - N_GROUPS,), F32)]).reshape(1, LANES)
        gates = _router(h2, w_rt, b_rt)
        hidden = _experts_hidden(h2, moe_w_gate[layer].astype(BF16), moe_w_up[layer].astype(BF16), gates)
        n_exp, f, _ = moe_w_down[layer].shape
        y = _matmul(hidden, moe_w_down[layer].reshape(n_exp * f, d).astype(BF16), F32, tn=512,
                    name="moe_down")
        nxt = min(layer + 1, depth - 1)
        xs, h = _resid_ln(xs, y, mods3, layer, 5, ln2_g[layer], ln2_b[layer], nxt, 1, 0, seq, alpha)
    return xs[:seq].reshape(batch, seq, d)
```

```python
import functools
import math

import jax
import jax.numpy as jnp
from jax import lax
from jax.experimental import pallas as pl
from jax.experimental.pallas import tpu as pltpu

F32 = jnp.float32
BF16 = jnp.bfloat16

GRID_W = 64
HEAD_DIM = 128
ROPE_THETA = 10000.0
LN_EPS = 1e-5
RMS_EPS = 1e-6
N_GROUPS = 4
EXPERTS_PER_GROUP = 4
N_EXPERTS = N_GROUPS * EXPERTS_PER_GROUP
N_MOD = 6
MLA_NOPE = 128
MLA_ROPE = 64
MLA_V = 128
MLA_HEAD_PAD = 256
LOG2E = math.log2(math.e)

LANES = 128
V7X_VMEM_LIMIT = 56 * 2**20

ROW_TILE = 256
MM_TM = 768
KV_CHUNK = 768
MOE_TM = 256
ROW_COPY_BLOCK = 16
ROW_COPY_LAG = 8


def _cparams(*sem):
    return pltpu.CompilerParams(dimension_semantics=sem, vmem_limit_bytes=V7X_VMEM_LIMIT)


def _ada_kernel(ct_ref, w_ref, b_ref, o_ref):
    ct = ct_ref[...]
    s = ct * jax.nn.sigmoid(ct)
    w = w_ref[...]
    b = b_ref[...]
    for r in range(2):
        o_ref[r:r + 1, :] = jnp.sum(w * s[:, r:r + 1], axis=0, keepdims=True) + b


def _ada_mods(c, c_ctx, w_ada, b_ada):
    depth, d, n = w_ada.shape
    tn = 512
    ct = jnp.stack([c.reshape(d), c_ctx.reshape(d)], axis=1)
    return pl.pallas_call(
        _ada_kernel,
        grid=(depth, n // tn),
        in_specs=[
            pl.BlockSpec((d, 2), lambda l, j: (0, 0)),
            pl.BlockSpec((None, d, tn), lambda l, j: (l, 0, j)),
            pl.BlockSpec((None, 1, tn), lambda l, j: (l, 0, j)),
        ],
        out_specs=pl.BlockSpec((None, 2, tn), lambda l, j: (l, 0, j)),
        out_shape=jax.ShapeDtypeStruct((depth, 2, n), F32),
        compiler_params=_cparams("arbitrary", "arbitrary"),
        name="ada_mods",
    )(ct, w_ada, b_ada.reshape(depth, 1, n))


def _mod_spec(layer, which, d, seq):
    def index(i):
        sel = jnp.where(i * ROW_TILE >= seq, 1, 0)
        return ((layer * 2 + sel) * N_MOD + which, 0, 0)
    return pl.BlockSpec((None, 1, d), index)


def _modulate_kernel(x_ref, sc_ref, sh_ref, h_ref):
    h_ref[...] = (x_ref[...] * (1.0 + sc_ref[...]) + sh_ref[...]).astype(h_ref.dtype)


def _modulate(x, mods3, layer, sc_idx, sh_idx, seq):
    m, d = x.shape
    return pl.pallas_call(
        _modulate_kernel,
        grid=(m // ROW_TILE,),
        in_specs=[pl.BlockSpec((ROW_TILE, d), lambda i: (i, 0)),
                  _mod_spec(layer, sc_idx, d, seq), _mod_spec(layer, sh_idx, d, seq)],
        out_specs=pl.BlockSpec((ROW_TILE, d), lambda i: (i, 0)),
        out_shape=jax.ShapeDtypeStruct((m, d), BF16),
        compiler_params=_cparams("arbitrary"),
        name="modulate",
    )(x, mods3, mods3)


def _resid_ln_kernel(x_ref, o_ref, g_ref, lg_ref, lb_ref, sc_ref, sh_ref, y_ref, h_ref, *, alpha):
    z = alpha * x_ref[...] + g_ref[...] * o_ref[...]
    mu = jnp.mean(z, axis=-1, keepdims=True)
    zc = z - mu
    var = jnp.mean(zc * zc, axis=-1, keepdims=True)
    y = zc * lax.rsqrt(var + LN_EPS) * lg_ref[...] + lb_ref[...]
    y_ref[...] = y
    h_ref[...] = (y * (1.0 + sc_ref[...]) + sh_ref[...]).astype(h_ref.dtype)


def _resid_ln(x, o, mods3, layer, gate_idx, ln_g, ln_b, nxt_layer, sc_idx, sh_idx, seq, alpha, h_dtype=BF16):
    m, d = x.shape
    row = pl.BlockSpec((ROW_TILE, d), lambda i: (i, 0))
    vec = pl.BlockSpec((1, d), lambda i: (0, 0))
    return pl.pallas_call(
        functools.partial(_resid_ln_kernel, alpha=alpha),
        grid=(m // ROW_TILE,),
        in_specs=[row, row, _mod_spec(layer, gate_idx, d, seq), vec, vec,
                  _mod_spec(nxt_layer, sc_idx, d, seq), _mod_spec(nxt_layer, sh_idx, d, seq)],
        out_specs=[row, row],
        out_shape=[jax.ShapeDtypeStruct((m, d), F32), jax.ShapeDtypeStruct((m, d), h_dtype)],
        compiler_params=_cparams("arbitrary"),
        name="resid_ln",
    )(x, o, mods3, ln_g.reshape(1, d), ln_b.reshape(1, d), mods3, mods3)


def _mm_kernel(*refs, epilogue, n_extra):
    a_ref, b_ref = refs[0], refs[1]
    extras = refs[2:2 + n_extra]
    o_ref = refs[2 + n_extra]
    acc = jnp.dot(a_ref[...], b_ref[...], preferred_element_type=F32)
    epilogue(acc, o_ref, *extras)


def _store(acc, o_ref):
    o_ref[...] = acc.astype(o_ref.dtype)


def _matmul(a, b, out_dtype, *, tn, epilogue=_store, extras=(), extra_specs=(), name="matmul"):
    m, k = a.shape
    n = b.shape[1]
    tm = MM_TM
    tn = min(tn, n)
    assert m % tm == 0 and n % tn == 0
    return pl.pallas_call(
        functools.partial(_mm_kernel, epilogue=epilogue, n_extra=len(extras)),
        grid=(n // tn, m // tm),
        in_specs=[pl.BlockSpec((tm, k), lambda j, i: (i, 0)),
                  pl.BlockSpec((k, tn), lambda j, i: (0, j)),
                  *extra_specs],
        out_specs=pl.BlockSpec((tm, tn), lambda j, i: (i, j)),
        out_shape=jax.ShapeDtypeStruct((m, n), out_dtype),
        compiler_params=_cparams("arbitrary", "arbitrary"),
        name=name,
    )(a, b, *extras)


def _rope_tables(seq, n_ctx, rot_dim):
    rows = seq // GRID_W
    row, col = jnp.meshgrid(jnp.arange(rows), jnp.arange(GRID_W), indexing="ij")
    row = row.reshape(-1).astype(F32)
    col = col.reshape(-1).astype(F32)
    quarter = rot_dim // 4
    inv_freq = ROPE_THETA ** (-jnp.arange(quarter, dtype=F32) / quarter)
    ang = jnp.concatenate([row[:, None] * inv_freq, col[:, None] * inv_freq], axis=-1)
    cos, sin = jnp.cos(ang), jnp.sin(ang)
    half = rot_dim // 2
    pad = LANES - rot_dim
    zeros_h = jnp.zeros((seq, half), F32)
    zeros_p = jnp.zeros((seq, pad), F32)
    cos_t = jnp.concatenate([cos, cos, zeros_p], axis=-1)
    sin_lo = jnp.concatenate([-sin, zeros_h, zeros_p], axis=-1)
    sin_hi = jnp.concatenate([zeros_h, sin, zeros_p], axis=-1)
    ident = jnp.concatenate([jnp.ones((n_ctx, rot_dim), F32), jnp.zeros((n_ctx, pad), F32)], axis=-1)
    zc = jnp.zeros((n_ctx, LANES), F32)
    return (jnp.concatenate([cos_t, ident], axis=0), jnp.concatenate([sin_lo, zc], axis=0),
            jnp.concatenate([sin_hi, zc], axis=0))


def _rope_block(x, cos, sin_lo, sin_hi, half):
    return (x * cos + pltpu.roll(x, LANES - half, axis=1) * sin_lo
            + pltpu.roll(x, half, axis=1) * sin_hi)


def _head_epilogue(acc, o_ref, *extras, norm, rope, scale):
    idx = 0
    if norm:
        g = extras[idx][...]
        idx += 1
    if rope:
        cos, sin_lo, sin_hi = (r[...] for r in extras[idx:idx + 3])
    for hb in range(acc.shape[1] // LANES):
        x = acc[:, hb * LANES:(hb + 1) * LANES]
        if norm:
            x = x * lax.rsqrt(jnp.mean(x * x, axis=-1, keepdims=True) + RMS_EPS) * g
        if rope:
            x = _rope_block(x, cos, sin_lo, sin_hi, HEAD_DIM // 2)
        if scale != 1.0:
            x = x * scale
        o_ref[:, hb * LANES:(hb + 1) * LANES] = x.astype(o_ref.dtype)


def _head_proj(h, w, *, tn, norm_g=None, tables=None, scale=1.0, name):
    extras, specs = [], []
    if norm_g is not None:
        extras.append(norm_g.reshape(1, LANES))
        specs.append(pl.BlockSpec((1, LANES), lambda j, i: (0, 0)))
    if tables is not None:
        extras.extend(tables)
        specs.extend([pl.BlockSpec((MM_TM, LANES), lambda j, i: (i, 0))] * 3)
    ep = functools.partial(_head_epilogue, norm=norm_g is not None, rope=tables is not None, scale=scale)
    return _matmul(h, w, BF16, tn=tn, epilogue=ep, extras=extras, extra_specs=specs, name=name)


def _rmsnorm_epilogue(acc, o_ref, g_ref):
    x = acc * lax.rsqrt(jnp.mean(acc * acc, axis=-1, keepdims=True) + RMS_EPS) * g_ref[...]
    o_ref[...] = x.astype(o_ref.dtype)


def _rope_m_epilogue(acc, o_ref, cos_ref, lo_ref, hi_ref):
    o_ref[...] = _rope_block(acc, cos_ref[...], lo_ref[...], hi_ref[...], MLA_ROPE // 2).astype(o_ref.dtype)


def _mla_q_epilogue(acc, o_ref, cos_ref, lo_ref, hi_ref, *, scale):
    cos, lo, hi = cos_ref[...], lo_ref[...], hi_ref[...]
    for hd in range(acc.shape[1] // MLA_HEAD_PAD):
        base = hd * MLA_HEAD_PAD
        o_ref[:, base:base + LANES] = (acc[:, base:base + LANES] * scale).astype(o_ref.dtype)
        r = _rope_block(acc[:, base + LANES:base + 2 * LANES], cos, lo, hi, MLA_ROPE // 2)
        o_ref[:, base + LANES:base + 2 * LANES] = (r * scale).astype(o_ref.dtype)


def _softmax_step(s, v, m_ref, l_ref, acc_ref):
    m_old = m_ref[...]
    m_new = jnp.maximum(m_old, jnp.max(s, axis=-1, keepdims=True))
    p = jnp.concatenate([jnp.exp2(s[:, j * LANES:(j + 1) * LANES] - m_new)
                         for j in range(s.shape[1] // LANES)], axis=1)
    alpha = jnp.exp2(m_old - m_new)
    if l_ref is not None:
        l_ref[...] = alpha * l_ref[...] + jnp.sum(p, axis=-1, keepdims=True)
    pv = jnp.dot(p.astype(BF16), v, preferred_element_type=F32)
    acc_ref[...] = jnp.concatenate([alpha] * (acc_ref.shape[1] // LANES), axis=1) * acc_ref[...] + pv
    m_ref[...] = m_new


def _flash(q, k_ref, v_ref, m_ref, l_ref, acc_ref, s_ref, tk):
    n_chunks = k_ref.shape[0] // tk
    m_ref[...] = jnp.full(m_ref.shape, -jnp.inf, F32)
    acc_ref[...] = jnp.zeros(acc_ref.shape, F32)
    if l_ref is not None:
        l_ref[...] = jnp.zeros(l_ref.shape, F32)
    nt_dims = (((1,), (1,)), ((), ()))
    s_ref[...] = lax.dot_general(q, k_ref[pl.ds(0, tk), :], nt_dims, preferred_element_type=F32)

    def body(c, carry):
        start = pl.multiple_of(c * tk, tk)
        nxt = pl.multiple_of((c + 1) * tk, tk)
        s = s_ref[...]
        s_next = lax.dot_general(q, k_ref[pl.ds(nxt, tk), :], nt_dims, preferred_element_type=F32)
        _softmax_step(s, v_ref[pl.ds(start, tk), :], m_ref, l_ref, acc_ref)
        s_ref[...] = s_next
        return carry

    lax.fori_loop(0, n_chunks - 1, body, 0)
    _softmax_step(s_ref[...], v_ref[pl.ds((n_chunks - 1) * tk, tk), :], m_ref, l_ref, acc_ref)


def _ones_behind(v_ref, vaug_ref):
    vaug_ref[:, :LANES] = v_ref[...]
    vaug_ref[:, LANES:] = jnp.ones((v_ref.shape[0], LANES), BF16)


def _diff_attn_kernel(lam_ref, g_ref, q1_ref, q2_ref, k1_ref, k2_ref, v_ref, o_ref,
                      m_ref, l_ref, acc_ref, s_ref, a1_ref, *, tk, lambda_init):
    lv = lam_ref[...]
    lam = (jnp.exp(jnp.sum(lv[0:1] * lv[1:2], axis=-1, keepdims=True))
           - jnp.exp(jnp.sum(lv[2:3] * lv[3:4], axis=-1, keepdims=True)) + lambda_init)
    n_rep = acc_ref.shape[1] // LANES
    _flash(q1_ref[...], k1_ref, v_ref, m_ref, l_ref, acc_ref, s_ref, tk)
    a1_ref[...] = acc_ref[...] / jnp.concatenate([l_ref[...]] * n_rep, axis=1)
    _flash(q2_ref[...], k2_ref, v_ref, m_ref, l_ref, acc_ref, s_ref, tk)
    o = a1_ref[...] - lam * (acc_ref[...] / jnp.concatenate([l_ref[...]] * n_rep, axis=1))
    o = o * lax.rsqrt(jnp.mean(o * o, axis=-1, keepdims=True) + RMS_EPS) * g_ref[...]
    o_ref[...] = (o * (1.0 - lambda_init)).astype(o_ref.dtype)


def _gqa_attn_kernel(q_ref, k_ref, v_ref, o_ref, vaug_ref, qs_ref, m_ref, acc_ref, s_ref, *, tk, rep):
    tq = q_ref.shape[0]

    @pl.when(pl.program_id(1) == 0)
    def _():
        _ones_behind(v_ref, vaug_ref)

    for r in range(rep):
        qs_ref[r * tq:(r + 1) * tq, :] = q_ref[:, r * LANES:(r + 1) * LANES]
    _flash(qs_ref[...], k_ref, vaug_ref, m_ref, None, acc_ref, s_ref, tk)
    for r in range(rep):
        rows = slice(r * tq, (r + 1) * tq)
        o_ref[:, r * LANES:(r + 1) * LANES] = (acc_ref[rows, :LANES] / acc_ref[rows, LANES:]).astype(o_ref.dtype)


def _mla_attn_kernel(q_ref, kn_ref, kr_ref, v_ref, o_ref, kcat_ref, vaug_ref, m_ref, acc_ref, s_ref, *, tk):
    @pl.when(pl.program_id(1) == 0)
    def _():
        kcat_ref[:, :LANES] = kn_ref[...]
        kcat_ref[:, LANES:] = kr_ref[...]
        _ones_behind(v_ref, vaug_ref)

    _flash(q_ref[...], kcat_ref, vaug_ref, m_ref, None, acc_ref, s_ref, tk)
    o_ref[...] = (acc_ref[:, :LANES] / acc_ref[:, LANES:]).astype(o_ref.dtype)


def _diff_attention(qa, ka, va, lam_vecs, subln_g, lambda_init, *, q_rows, kv_rows, tq, tk):
    n_heads = va.shape[1] // (2 * HEAD_DIM)
    q0, nq = q_rows
    k0, nk = kv_rows
    qb, kb = q0 // tq, k0 // nk
    dv = 2 * HEAD_DIM
    return pl.pallas_call(
        functools.partial(_diff_attn_kernel, tk=tk, lambda_init=lambda_init),
        grid=(n_heads, nq // tq),
        in_specs=[
            pl.BlockSpec((4, HEAD_DIM), lambda h, i: (0, 0)),
            pl.BlockSpec((1, dv), lambda h, i: (0, 0)),
            pl.BlockSpec((tq, HEAD_DIM), lambda h, i: (qb + i, 2 * h)),
            pl.BlockSpec((tq, HEAD_DIM), lambda h, i: (qb + i, 2 * h + 1)),
            pl.BlockSpec((nk, HEAD_DIM), lambda h, i: (kb, 2 * h)),
            pl.BlockSpec((nk, HEAD_DIM), lambda h, i: (kb, 2 * h + 1)),
            pl.BlockSpec((nk, dv), lambda h, i: (kb, h)),
        ],
        out_specs=pl.BlockSpec((tq, dv), lambda h, i: (i, h)),
        out_shape=jax.ShapeDtypeStruct((nq, n_heads * dv), BF16),
        scratch_shapes=[pltpu.VMEM((tq, LANES), F32), pltpu.VMEM((tq, LANES), F32), pltpu.VMEM((tq, dv), F32),
                        pltpu.VMEM((tq, tk), F32), pltpu.VMEM((tq, dv), F32)],
        compiler_params=_cparams("arbitrary", "arbitrary"),
        name="diff_attention",
    )(lam_vecs, subln_g.reshape(1, dv), qa, qa, ka, ka, va)


def _gqa_attention(qg, kg, vg, *, q_rows, kv_rows, tq, tk):
    n_kv = kg.shape[1] // HEAD_DIM
    rep = qg.shape[1] // kg.shape[1]
    q0, nq = q_rows
    k0, nk = kv_rows
    qb, kb = q0 // tq, k0 // nk
    rows = rep * tq
    return pl.pallas_call(
        functools.partial(_gqa_attn_kernel, tk=tk, rep=rep),
        grid=(n_kv, nq // tq),
        in_specs=[
            pl.BlockSpec((tq, rep * HEAD_DIM), lambda g, i: (qb + i, g)),
            pl.BlockSpec((nk, HEAD_DIM), lambda g, i: (kb, g)),
            pl.BlockSpec((nk, HEAD_DIM), lambda g, i: (kb, g)),
        ],
        out_specs=pl.BlockSpec((tq, rep * HEAD_DIM), lambda g, i: (i, g)),
        out_shape=jax.ShapeDtypeStruct((nq, qg.shape[1]), BF16),
        scratch_shapes=[pltpu.VMEM((nk, 2 * LANES), BF16), pltpu.VMEM((rows, HEAD_DIM), BF16),
                        pltpu.VMEM((rows, LANES), F32), pltpu.VMEM((rows, 2 * LANES), F32),
                        pltpu.VMEM((rows, tk), F32)],
        compiler_params=_cparams("arbitrary", "arbitrary"),
        name="gqa_attention",
    )(qg, kg, vg)


def _mla_attention(q, kv, kr, *, q_rows, kv_rows, tq, tk):
    n_heads = q.shape[1] // MLA_HEAD_PAD
    q0, nq = q_rows
    k0, nk = kv_rows
    qb, kb = q0 // tq, k0 // nk
    return pl.pallas_call(
        functools.partial(_mla_attn_kernel, tk=tk),
        grid=(n_heads, nq // tq),
        in_specs=[
            pl.BlockSpec((tq, MLA_HEAD_PAD), lambda h, i: (qb + i, h)),
            pl.BlockSpec((nk, MLA_NOPE), lambda h, i: (kb, 2 * h)),
            pl.BlockSpec((nk, LANES), lambda h, i: (kb, 0)),
            pl.BlockSpec((nk, MLA_V), lambda h, i: (kb, 2 * h + 1)),
        ],
        out_specs=pl.BlockSpec((tq, MLA_V), lambda h, i: (i, h)),
        out_shape=jax.ShapeDtypeStruct((nq, n_heads * MLA_V), BF16),
        scratch_shapes=[pltpu.VMEM((nk, MLA_HEAD_PAD), BF16), pltpu.VMEM((nk, 2 * LANES), BF16),
                        pltpu.VMEM((tq, LANES), F32), pltpu.VMEM((tq, 2 * LANES), F32),
                        pltpu.VMEM((tq, tk), F32)],
        compiler_params=_cparams("arbitrary", "arbitrary"),
        name="mla_attention",
    )(q, kv, kr, kv)


def _router_kernel(h_ref, w_ref, b_ref, g_ref):
    logits = jnp.dot(h_ref[...].astype(BF16), w_ref[...], preferred_element_type=F32) + b_ref[...]
    lane = lax.broadcasted_iota(jnp.int32, logits.shape, 1).astype(F32)
    neg = jnp.float32(-jnp.inf)
    big = jnp.float32(LANES)

    gmask = (lane >= N_EXPERTS) & (lane < N_EXPERTS + N_GROUPS)
    gl = jnp.where(gmask, logits, neg)
    ge = jnp.exp(gl - jnp.max(gl, axis=-1, keepdims=True))
    gprob = ge / jnp.sum(ge, axis=-1, keepdims=True)
    group_w = jnp.max(gprob, axis=-1, keepdims=True)
    gidx = jnp.min(jnp.where(gmask & (gprob == group_w), lane, big), axis=-1, keepdims=True) - N_EXPERTS

    emask = (lane >= gidx * EXPERTS_PER_GROUP) & (lane < (gidx + 1) * EXPERTS_PER_GROUP)
    el = jnp.where(emask, logits, neg)
    ee = jnp.exp(el - jnp.max(el, axis=-1, keepdims=True))
    ep = ee / jnp.sum(ee, axis=-1, keepdims=True)
    p1 = jnp.max(ep, axis=-1, keepdims=True)
    i1 = jnp.min(jnp.where(emask & (ep == p1), lane, big), axis=-1, keepdims=True)
    rest = emask & (lane != i1)
    p2 = jnp.max(jnp.where(rest, ep, neg), axis=-1, keepdims=True)
    i2 = jnp.min(jnp.where(rest & (ep == p2), lane, big), axis=-1, keepdims=True)
    denom = p1 + p2
    g_ref[...] = (jnp.where(lane == i1, group_w * (p1 / denom), 0.0)
                  + jnp.where(lane == i2, group_w * (p2 / denom), 0.0)
                  + jnp.where(lane == N_EXPERTS, gidx, 0.0))


def _router(h2, w_rt, b_rt):
    m, d = h2.shape
    return pl.pallas_call(
        _router_kernel,
        grid=(m // ROW_TILE,),
        in_specs=[pl.BlockSpec((ROW_TILE, d), lambda i: (i, 0)),
                  pl.BlockSpec((d, LANES), lambda i: (0, 0)),
                  pl.BlockSpec((1, LANES), lambda i: (0, 0))],
        out_specs=pl.BlockSpec((ROW_TILE, LANES), lambda i: (i, 0)),
        out_shape=jax.ShapeDtypeStruct((m, LANES), F32),
        compiler_params=_cparams("arbitrary"),
        name="moe_router",
    )(h2, w_rt, b_rt)


def _routing_plan(group, tm, n_tiles):
    onehot = (group[:, None] == jnp.arange(N_GROUPS, dtype=jnp.int32)[None, :]).astype(jnp.int32)
    padded = (jnp.sum(onehot, axis=0) + tm - 1) // tm * tm
    ends = jnp.cumsum(padded)
    rank = jnp.take_along_axis(jnp.cumsum(onehot, axis=0), group[:, None], axis=1)[:, 0] - 1
    dest = (ends - padded)[group] + rank
    tile_group = jnp.searchsorted(ends, jnp.arange(n_tiles, dtype=jnp.int32) * tm, side="right")
    tile_group = jnp.minimum(tile_group, N_GROUPS - 1)
    return dest.astype(jnp.int32), tile_group.astype(jnp.int32), (ends[-1:] // tm).astype(jnp.int32)


def _row_copy_kernel(si_ref, di_ref, *refs, n_rows, n_streams):
    srcs = refs[:n_streams]
    dsts = refs[2 * n_streams:3 * n_streams]
    sems = refs[3 * n_streams]
    n_steps = n_rows // ROW_COPY_BLOCK

    def copies(n):
        return [pltpu.make_async_copy(srcs[j].at[pl.ds(si_ref[n], 1)], dsts[j].at[pl.ds(di_ref[n], 1)],
                                      sems.at[j]) for j in range(n_streams)]

    def wait_block():
        for _ in range(ROW_COPY_BLOCK):
            for cp in copies(0):
                cp.wait()

    def body(i, carry):
        for u in range(ROW_COPY_BLOCK):
            for cp in copies(i * ROW_COPY_BLOCK + u):
                cp.start()

        @pl.when(i >= ROW_COPY_LAG)
        def _():
            wait_block()
        return carry

    lax.fori_loop(0, n_steps, body, 0)
    for _ in range(min(ROW_COPY_LAG, n_steps)):
        wait_block()


def _row_copy(src_idx, dst_idx, srcs, inits):
    n_rows = src_idx.shape[0]
    assert n_rows % ROW_COPY_BLOCK == 0
    k = len(srcs)
    any_spec = pl.BlockSpec(memory_space=pl.ANY)
    return pl.pallas_call(
        functools.partial(_row_copy_kernel, n_rows=n_rows, n_streams=k),
        grid_spec=pltpu.PrefetchScalarGridSpec(
            num_scalar_prefetch=2, grid=(1,),
            in_specs=[any_spec] * (2 * k), out_specs=[any_spec] * k,
            scratch_shapes=[pltpu.SemaphoreType.DMA((k,))]),
        out_shape=[jax.ShapeDtypeStruct(a.shape, a.dtype) for a in inits],
        input_output_aliases={2 + k + j: j for j in range(k)},
        compiler_params=pltpu.CompilerParams(dimension_semantics=("arbitrary",)),
        name="row_copy",
    )(src_idx, dst_idx, *srcs, *inits)


def _group_up_kernel(tg_ref, nt_ref, x_ref, wg_ref, wu_ref, gates_ref, o_ref, *, f):
    nh, t = pl.program_id(0), pl.program_id(1)

    @pl.when(t < nt_ref[0])
    def _():
        x = x_ref[...].astype(BF16)
        g = jnp.dot(x, wg_ref[...], preferred_element_type=F32)
        u = jnp.dot(x, wu_ref[...], preferred_element_type=F32)
        hid = g * jax.nn.sigmoid(g) * u
        gates = gates_ref[...]
        lane = lax.broadcasted_iota(jnp.int32, gates.shape, 1)
        n_local = hid.shape[1] // f
        for e in range(n_local):
            expert = tg_ref[t] * EXPERTS_PER_GROUP + nh * n_local + e
            gate = jnp.sum(jnp.where(lane == expert, gates, 0.0), axis=-1, keepdims=True)
            o_ref[:, e * f:(e + 1) * f] = (hid[:, e * f:(e + 1) * f] * gate).astype(o_ref.dtype)


def _group_up(tile_group, n_used, xs, w_gate_g, w_up_g, gates_s, f):
    mp, d = xs.shape
    n_wide = w_gate_g.shape[2]
    tn = n_wide // 2
    tm = MOE_TM
    wspec = pl.BlockSpec((None, d, tn), lambda nh, t, tg, nt: (tg[t], 0, nh))
    return pl.pallas_call(
        functools.partial(_group_up_kernel, f=f),
        grid_spec=pltpu.PrefetchScalarGridSpec(
            num_scalar_prefetch=2, grid=(n_wide // tn, mp // tm),
            in_specs=[pl.BlockSpec((tm, d), lambda nh, t, tg, nt: (t, 0)), wspec, wspec,
                      pl.BlockSpec((tm, LANES), lambda nh, t, tg, nt: (t, 0))],
            out_specs=pl.BlockSpec((tm, tn), lambda nh, t, tg, nt: (t, nh))),
        out_shape=jax.ShapeDtypeStruct((mp, n_wide), BF16),
        compiler_params=_cparams("arbitrary", "arbitrary"),
        name="moe_group_up",
    )(tile_group, n_used, xs, w_gate_g, w_up_g, gates_s)


def _group_down_kernel(tg_ref, nt_ref, h_ref, w_ref, o_ref):
    @pl.when(pl.program_id(0) < nt_ref[0])
    def _():
        o_ref[...] = jnp.dot(h_ref[...], w_ref[...], preferred_element_type=F32)


def _group_down(tile_group, n_used, hidden_s, w_down_g):
    mp, n_wide = hidden_s.shape
    d = w_down_g.shape[2]
    tm = MOE_TM
    return pl.pallas_call(
        _group_down_kernel,
        grid_spec=pltpu.PrefetchScalarGridSpec(
            num_scalar_prefetch=2, grid=(mp // tm,),
            in_specs=[pl.BlockSpec((tm, n_wide), lambda t, tg, nt: (t, 0)),
                      pl.BlockSpec((None, n_wide, d), lambda t, tg, nt: (tg[t], 0, 0))],
            out_specs=pl.BlockSpec((tm, d), lambda t, tg, nt: (t, 0))),
        out_shape=jax.ShapeDtypeStruct((mp, d), F32),
        compiler_params=_cparams("arbitrary"),
        name="moe_group_down",
    )(tile_group, n_used, hidden_s, w_down_g)


def _routed_experts(h2, gates, w_gate, w_up, w_down):
    m, d = h2.shape
    n_exp, _, f = w_gate.shape
    tm = MOE_TM
    mp = m + N_GROUPS * tm
    group = gates[:, N_EXPERTS].astype(jnp.int32)
    dest, tile_group, n_used = _routing_plan(group, tm, mp // tm)
    rows = jnp.arange(m, dtype=jnp.int32)
    xs, gates_s = _row_copy(rows, dest, [h2, gates], [jnp.zeros((mp, d), F32), jnp.zeros((mp, LANES), F32)])

    def by_group(w):
        return (w.reshape(N_GROUPS, EXPERTS_PER_GROUP, d, f).transpose(0, 2, 1, 3)
                .reshape(N_GROUPS, d, EXPERTS_PER_GROUP * f).astype(BF16))

    hidden_s = _group_up(tile_group, n_used, xs, by_group(w_gate), by_group(w_up), gates_s, f)
    y_s = _group_down(tile_group, n_used, hidden_s,
                      w_down.reshape(N_GROUPS, EXPERTS_PER_GROUP * f, d).astype(BF16))
    (y,) = _row_copy(dest, rows, [y_s], [jnp.zeros((m, d), F32)])
    return y


def kernel(x, c, ctx, c_ctx, w_ada, b_ada, ln1_g, ln1_b, ln2_g, ln2_b, ev_w_in, ev_w_out, diff_lambda, diff_subln_g, gqa_q_norm_g, gqa_k_norm_g, od_w_in, mla_q_norm_g, mla_kv_norm_g, mla_w_uq, mla_w_ukv, od_w_out, moe_w_group, moe_b_group, moe_w_router, moe_b_router, moe_w_gate, moe_w_up, moe_w_down):
    batch, seq, d = x.shape
    n_ctx = ctx.shape[1]
    depth = w_ada.shape[0]
    assert batch == 1 and seq % 1024 == 0 and n_ctx == ROW_TILE and (seq + n_ctx) % MM_TM == 0
    m = seq + n_ctx
    alpha = (2.0 * depth) ** 0.25
    lat_rows, ctx_rows = (0, seq), (seq, n_ctx)
    all_rows = (0, m)

    xs = jnp.concatenate([x.reshape(seq, d), ctx.reshape(n_ctx, d)], axis=0)
    mods3 = _ada_mods(c, c_ctx, w_ada, b_ada).reshape(depth * 2 * N_MOD, 1, d)
    tab_a = _rope_tables(seq, n_ctx, HEAD_DIM)
    tab_m = _rope_tables(seq, n_ctx, MLA_ROPE)

    h = _modulate(xs, mods3, 0, 1, 0, seq)
    for layer in range(depth):
        i = layer // 2
        if layer % 2 == 0:
            lambda_init = 0.8 - 0.6 * math.exp(-0.3 * layer)
            scale = LOG2E / math.sqrt(HEAD_DIM)
            w_in = ev_w_in[i]
            n_diff_qk = d // 2
            n_diff_v = d // 2
            n_gqa_q = d // 2
            n_gqa_kv = n_gqa_q // 4
            o0 = 0
            segs = []
            for width in (n_diff_qk, n_diff_qk, n_diff_v, n_gqa_q, n_gqa_kv, n_gqa_kv):
                segs.append(w_in[:, o0:o0 + width].astype(BF16))
                o0 += width
            qa = _head_proj(h, segs[0], tn=512, tables=tab_a, scale=scale, name="proj_diff_q")
            ka = _head_proj(h, segs[1], tn=512, tables=tab_a, name="proj_diff_k")
            va = _matmul(h, segs[2], BF16, tn=512, name="proj_diff_v")
            qg = _head_proj(h, segs[3], tn=512, norm_g=gqa_q_norm_g[i], tables=tab_a, scale=scale,
                            name="proj_gqa_q")
            kg = _head_proj(h, segs[4], tn=512, norm_g=gqa_k_norm_g[i], tables=tab_a, name="proj_gqa_k")
            vg = _matmul(h, segs[5], BF16, tn=512, name="proj_gqa_v")
            parts = []
            for q_rows, kv_rows, tq_d, tq_g, tk in ((lat_rows, all_rows, 1024, 256, KV_CHUNK),
                                                    (ctx_rows, ctx_rows, n_ctx, n_ctx, n_ctx)):
                od = _diff_attention(qa, ka, va, diff_lambda[i], diff_subln_g[i], lambda_init,
                                     q_rows=q_rows, kv_rows=kv_rows, tq=tq_d, tk=tk)
                og = _gqa_attention(qg, kg, vg, q_rows=q_rows, kv_rows=kv_rows, tq=tq_g, tk=tk)
                parts.append(jnp.concatenate([od, og], axis=1))
            attn = jnp.concatenate(parts, axis=0)
            w_out = ev_w_out[i].astype(BF16)
        else:
            scale = LOG2E / math.sqrt(MLA_NOPE + MLA_ROPE)
            w_in = od_w_in[i]
            q_rank = mla_q_norm_g.shape[1]
            kv_rank = mla_kv_norm_g.shape[1]
            n_heads = mla_w_uq.shape[2] // (MLA_NOPE + MLA_ROPE)
            vec = lambda n: pl.BlockSpec((1, n), lambda j, i_: (0, 0))
            tab_specs = [pl.BlockSpec((MM_TM, LANES), lambda j, i_: (i_, 0))] * 3
            cq = _matmul(h, w_in[:, :q_rank].astype(BF16), BF16, tn=q_rank, epilogue=_rmsnorm_epilogue,
                         extras=[mla_q_norm_g[i].reshape(1, q_rank)], extra_specs=[vec(q_rank)],
                         name="proj_mla_cq")
            ckv = _matmul(h, w_in[:, q_rank:q_rank + kv_rank].astype(BF16), BF16, tn=kv_rank,
                          epilogue=_rmsnorm_epilogue, extras=[mla_kv_norm_g[i].reshape(1, kv_rank)],
                          extra_specs=[vec(kv_rank)], name="proj_mla_ckv")
            w_kr = jnp.pad(w_in[:, q_rank + kv_rank:], ((0, 0), (0, LANES - MLA_ROPE))).astype(BF16)
            kr = _matmul(h, w_kr, BF16, tn=LANES, epilogue=_rope_m_epilogue, extras=tab_m,
                         extra_specs=tab_specs, name="proj_mla_krope")
            w_uq = jnp.pad(mla_w_uq[i].reshape(q_rank, n_heads, MLA_NOPE + MLA_ROPE),
                           ((0, 0), (0, 0), (0, MLA_HEAD_PAD - MLA_NOPE - MLA_ROPE)))
            w_uq = w_uq.reshape(q_rank, n_heads * MLA_HEAD_PAD).astype(BF16)
            q = _matmul(cq, w_uq, BF16, tn=1024, epilogue=functools.partial(_mla_q_epilogue, scale=scale),
                        extras=tab_m, extra_specs=tab_specs, name="proj_mla_q")
            kv = _matmul(ckv, mla_w_ukv[i].astype(BF16), BF16, tn=1024, name="proj_mla_kv")
            attn = jnp.concatenate([
                _mla_attention(q, kv, kr, q_rows=lat_rows, kv_rows=all_rows, tq=1024, tk=KV_CHUNK),
                _mla_attention(q, kv, kr, q_rows=ctx_rows, kv_rows=ctx_rows, tq=n_ctx, tk=n_ctx)], axis=0)
            w_out = od_w_out[i].astype(BF16)

        o = _matmul(attn, w_out, F32, tn=1024, name="proj_out")
        xs, h2 = _resid_ln(xs, o, mods3, layer, 2, ln1_g[layer], ln1_b[layer], layer, 4, 3, seq, alpha,
                           h_dtype=F32)

        w_rt = jnp.concatenate([moe_w_router[layer], moe_w_group[layer],
                                jnp.zeros((d, LANES - N_EXPERTS - N_GROUPS), F32)], axis=1).astype(BF16)
        b_rt = jnp.concatenate([moe_b_router[layer], moe_b_group[layer],
                                jnp.zeros((LANES - N_EXPERTS - N_GROUPS,), F32)]).reshape(1, LANES)
        gates = _router(h2, w_rt, b_rt)
        y = _routed_experts(h2, gates, moe_w_gate[layer], moe_w_up[layer], moe_w_down[layer])
        nxt = min(layer + 1, depth - 1)
        xs, h = _resid_ln(xs, y, mods3, layer, 5, ln2_g[layer], ln2_b[layer], nxt, 1, 0, seq, alpha)
    return xs[:seq].reshape(batch, seq, d)
```

```python
import functools
import math

import jax
import jax.numpy as jnp
from jax import lax
from jax.experimental import pallas as pl
from jax.experimental.pallas import tpu as pltpu

F32 = jnp.float32
BF16 = jnp.bfloat16

GRID_W = 64
HEAD_DIM = 128
ROPE_THETA = 10000.0
LN_EPS = 1e-5
RMS_EPS = 1e-6
N_GROUPS = 4
EXPERTS_PER_GROUP = 4
N_EXPERTS = N_GROUPS * EXPERTS_PER_GROUP
N_MOD = 6
MLA_NOPE = 128
MLA_ROPE = 64
MLA_V = 128
MLA_HEAD_PAD = 256
LOG2E = math.log2(math.e)

LANES = 128
F32_SUBLANES = 8
BF16_SUBLANES = 16
V7X_VMEM_LIMIT = 56 * 2**20

ROW_TILE = 256
MM_TM = 768
KV_CHUNK = 768
ATTN_ROWS = 2048
MOE_TM = 256
ROW_COPY_BLOCK = 16
ROW_COPY_LAG = 8


def _cparams(*sem):
    return pltpu.CompilerParams(dimension_semantics=sem, vmem_limit_bytes=V7X_VMEM_LIMIT)


def _ada_kernel(ct_ref, w_ref, b_ref, o_ref):
    ct = ct_ref[...]
    s = ct * jax.nn.sigmoid(ct)
    w = w_ref[...]
    b = b_ref[...]
    for r in range(2):
        o_ref[r:r + 1, :] = jnp.sum(w * s[:, r:r + 1], axis=0, keepdims=True) + b


def _ada_mods(c, c_ctx, w_ada, b_ada):
    depth, d, n = w_ada.shape
    tn = 512
    ct = jnp.stack([c.reshape(d), c_ctx.reshape(d)], axis=1)
    return pl.pallas_call(
        _ada_kernel,
        grid=(depth, n // tn),
        in_specs=[
            pl.BlockSpec((d, 2), lambda l, j: (0, 0)),
            pl.BlockSpec((None, d, tn), lambda l, j: (l, 0, j)),
            pl.BlockSpec((None, 1, tn), lambda l, j: (l, 0, j)),
        ],
        out_specs=pl.BlockSpec((None, 2, tn), lambda l, j: (l, 0, j)),
        out_shape=jax.ShapeDtypeStruct((depth, 2, n), F32),
        compiler_params=_cparams("arbitrary", "arbitrary"),
        name="ada_mods",
    )(ct, w_ada, b_ada.reshape(depth, 1, n))


def _mod_spec(layer, which, d, seq):
    def index(i):
        sel = jnp.where(i * ROW_TILE >= seq, 1, 0)
        return ((layer * 2 + sel) * N_MOD + which, 0, 0)
    return pl.BlockSpec((None, 1, d), index)


def _modulate_kernel(x_ref, sc_ref, sh_ref, h_ref):
    h_ref[...] = (x_ref[...] * (1.0 + sc_ref[...]) + sh_ref[...]).astype(h_ref.dtype)


def _modulate(x, mods3, layer, sc_idx, sh_idx, seq):
    m, d = x.shape
    return pl.pallas_call(
        _modulate_kernel,
        grid=(m // ROW_TILE,),
        in_specs=[pl.BlockSpec((ROW_TILE, d), lambda i: (i, 0)),
                  _mod_spec(layer, sc_idx, d, seq), _mod_spec(layer, sh_idx, d, seq)],
        out_specs=pl.BlockSpec((ROW_TILE, d), lambda i: (i, 0)),
        out_shape=jax.ShapeDtypeStruct((m, d), BF16),
        compiler_params=_cparams("arbitrary"),
        name="modulate",
    )(x, mods3, mods3)


def _resid_ln_kernel(x_ref, o_ref, g_ref, lg_ref, lb_ref, sc_ref, sh_ref, y_ref, h_ref, *, alpha):
    z = alpha * x_ref[...] + g_ref[...] * o_ref[...]
    mu = jnp.mean(z, axis=-1, keepdims=True)
    zc = z - mu
    var = jnp.mean(zc * zc, axis=-1, keepdims=True)
    y = zc * lax.rsqrt(var + LN_EPS) * lg_ref[...] + lb_ref[...]
    y_ref[...] = y
    h_ref[...] = (y * (1.0 + sc_ref[...]) + sh_ref[...]).astype(h_ref.dtype)


def _resid_ln(x, o, mods3, layer, gate_idx, ln_g, ln_b, nxt_layer, sc_idx, sh_idx, seq, alpha, h_dtype=BF16):
    m, d = x.shape
    row = pl.BlockSpec((ROW_TILE, d), lambda i: (i, 0))
    vec = pl.BlockSpec((1, d), lambda i: (0, 0))
    return pl.pallas_call(
        functools.partial(_resid_ln_kernel, alpha=alpha),
        grid=(m // ROW_TILE,),
        in_specs=[row, row, _mod_spec(layer, gate_idx, d, seq), vec, vec,
                  _mod_spec(nxt_layer, sc_idx, d, seq), _mod_spec(nxt_layer, sh_idx, d, seq)],
        out_specs=[row, row],
        out_shape=[jax.ShapeDtypeStruct((m, d), F32), jax.ShapeDtypeStruct((m, d), h_dtype)],
        compiler_params=_cparams("arbitrary"),
        name="resid_ln",
    )(x, o, mods3, ln_g.reshape(1, d), ln_b.reshape(1, d), mods3, mods3)


def _mm_kernel(*refs, epilogue, n_extra):
    a_ref, b_ref = refs[0], refs[1]
    extras = refs[2:2 + n_extra]
    o_ref = refs[2 + n_extra]
    acc = jnp.dot(a_ref[...], b_ref[...], preferred_element_type=F32)
    epilogue(acc, o_ref, *extras)


def _store(acc, o_ref):
    o_ref[...] = acc.astype(o_ref.dtype)


def _matmul(a, b, out_dtype, *, tn, epilogue=_store, extras=(), extra_specs=(), name="matmul"):
    m, k = a.shape
    n = b.shape[1]
    tm = MM_TM
    tn = min(tn, n)
    assert m % tm == 0 and n % tn == 0
    return pl.pallas_call(
        functools.partial(_mm_kernel, epilogue=epilogue, n_extra=len(extras)),
        grid=(n // tn, m // tm),
        in_specs=[pl.BlockSpec((tm, k), lambda j, i: (i, 0)),
                  pl.BlockSpec((k, tn), lambda j, i: (0, j)),
                  *extra_specs],
        out_specs=pl.BlockSpec((tm, tn), lambda j, i: (i, j)),
        out_shape=jax.ShapeDtypeStruct((m, n), out_dtype),
        compiler_params=_cparams("arbitrary", "arbitrary"),
        name=name,
    )(a, b, *extras)


def _rope_tables(seq, n_ctx, rot_dim):
    rows = seq // GRID_W
    row, col = jnp.meshgrid(jnp.arange(rows), jnp.arange(GRID_W), indexing="ij")
    row = row.reshape(-1).astype(F32)
    col = col.reshape(-1).astype(F32)
    quarter = rot_dim // 4
    inv_freq = ROPE_THETA ** (-jnp.arange(quarter, dtype=F32) / quarter)
    ang = jnp.concatenate([row[:, None] * inv_freq, col[:, None] * inv_freq], axis=-1)
    cos, sin = jnp.cos(ang), jnp.sin(ang)
    half = rot_dim // 2
    pad = LANES - rot_dim
    zeros_h = jnp.zeros((seq, half), F32)
    zeros_p = jnp.zeros((seq, pad), F32)
    cos_t = jnp.concatenate([cos, cos, zeros_p], axis=-1)
    sin_lo = jnp.concatenate([-sin, zeros_h, zeros_p], axis=-1)
    sin_hi = jnp.concatenate([zeros_h, sin, zeros_p], axis=-1)
    ident = jnp.concatenate([jnp.ones((n_ctx, rot_dim), F32), jnp.zeros((n_ctx, pad), F32)], axis=-1)
    zc = jnp.zeros((n_ctx, LANES), F32)
    return (jnp.concatenate([cos_t, ident], axis=0), jnp.concatenate([sin_lo, zc], axis=0),
            jnp.concatenate([sin_hi, zc], axis=0))


def _rope_block(x, cos, sin_lo, sin_hi, half):
    return (x * cos + pltpu.roll(x, LANES - half, axis=1) * sin_lo
            + pltpu.roll(x, half, axis=1) * sin_hi)


def _head_epilogue(acc, o_ref, *extras, norm, rope, scale):
    idx = 0
    if norm:
        g = extras[idx][...]
        idx += 1
    if rope:
        cos, sin_lo, sin_hi = (r[...] for r in extras[idx:idx + 3])
    for hb in range(acc.shape[1] // LANES):
        x = acc[:, hb * LANES:(hb + 1) * LANES]
        if norm:
            x = x * lax.rsqrt(jnp.mean(x * x, axis=-1, keepdims=True) + RMS_EPS) * g
        if rope:
            x = _rope_block(x, cos, sin_lo, sin_hi, HEAD_DIM // 2)
        if scale != 1.0:
            x = x * scale
        o_ref[:, hb * LANES:(hb + 1) * LANES] = x.astype(o_ref.dtype)


def _head_proj(h, w, *, tn, norm_g=None, tables=None, scale=1.0, name):
    extras, specs = [], []
    if norm_g is not None:
        extras.append(norm_g.reshape(1, LANES))
        specs.append(pl.BlockSpec((1, LANES), lambda j, i: (0, 0)))
    if tables is not None:
        extras.extend(tables)
        specs.extend([pl.BlockSpec((MM_TM, LANES), lambda j, i: (i, 0))] * 3)
    ep = functools.partial(_head_epilogue, norm=norm_g is not None, rope=tables is not None, scale=scale)
    return _matmul(h, w, BF16, tn=tn, epilogue=ep, extras=extras, extra_specs=specs, name=name)


def _rmsnorm_epilogue(acc, o_ref, g_ref):
    x = acc * lax.rsqrt(jnp.mean(acc * acc, axis=-1, keepdims=True) + RMS_EPS) * g_ref[...]
    o_ref[...] = x.astype(o_ref.dtype)


def _rope_m_epilogue(acc, o_ref, cos_ref, lo_ref, hi_ref):
    o_ref[...] = _rope_block(acc, cos_ref[...], lo_ref[...], hi_ref[...], MLA_ROPE // 2).astype(o_ref.dtype)


def _mla_q_epilogue(acc, o_ref, cos_ref, lo_ref, hi_ref, *, scale):
    cos, lo, hi = cos_ref[...], lo_ref[...], hi_ref[...]
    for hd in range(acc.shape[1] // MLA_HEAD_PAD):
        base = hd * MLA_HEAD_PAD
        o_ref[:, base:base + LANES] = (acc[:, base:base + LANES] * scale).astype(o_ref.dtype)
        r = _rope_block(acc[:, base + LANES:base + 2 * LANES], cos, lo, hi, MLA_ROPE // 2)
        o_ref[:, base + LANES:base + 2 * LANES] = (r * scale).astype(o_ref.dtype)


def _softmax_step(s, v, m_ref, l_ref, acc_ref):
    m_old = m_ref[...]
    m_new = jnp.maximum(m_old, jnp.max(s, axis=-1, keepdims=True))
    p = jnp.concatenate([jnp.exp2(s[:, j * LANES:(j + 1) * LANES] - m_new)
                         for j in range(s.shape[1] // LANES)], axis=1)
    alpha = jnp.exp2(m_old - m_new)
    if l_ref is not None:
        l_ref[...] = alpha * l_ref[...] + jnp.sum(p, axis=-1, keepdims=True)
    pv = jnp.dot(p.astype(BF16), v, preferred_element_type=F32)
    acc_ref[...] = jnp.concatenate([alpha] * (acc_ref.shape[1] // LANES), axis=1) * acc_ref[...] + pv
    m_ref[...] = m_new


def _flash(q, k_ref, v_ref, m_ref, l_ref, acc_ref, s_ref, tk):
    n_chunks = k_ref.shape[0] // tk
    m_ref[...] = jnp.full(m_ref.shape, -jnp.inf, F32)
    acc_ref[...] = jnp.zeros(acc_ref.shape, F32)
    if l_ref is not None:
        l_ref[...] = jnp.zeros(l_ref.shape, F32)
    nt_dims = (((1,), (1,)), ((), ()))
    s_ref[...] = lax.dot_general(q, k_ref[pl.ds(0, tk), :], nt_dims, preferred_element_type=F32)

    def body(c, carry):
        start = pl.multiple_of(c * tk, tk)
        nxt = pl.multiple_of((c + 1) * tk, tk)
        s = s_ref[...]
        s_next = lax.dot_general(q, k_ref[pl.ds(nxt, tk), :], nt_dims, preferred_element_type=F32)
        _softmax_step(s, v_ref[pl.ds(start, tk), :], m_ref, l_ref, acc_ref)
        s_ref[...] = s_next
        return carry

    lax.fori_loop(0, n_chunks - 1, body, 0)
    _softmax_step(s_ref[...], v_ref[pl.ds((n_chunks - 1) * tk, tk), :], m_ref, l_ref, acc_ref)


def _ones_behind(v_ref, vaug_ref):
    vaug_ref[:, :LANES] = v_ref[...]
    vaug_ref[:, LANES:] = jnp.ones((v_ref.shape[0], LANES), BF16)


def _diff_attn_kernel(lam_ref, g_ref, q1_ref, q2_ref, k1_ref, k2_ref, v_ref, o_ref,
                      m_ref, l_ref, acc_ref, s_ref, a1_ref, *, tk, lambda_init):
    lv = lam_ref[...]
    lam = (jnp.exp(jnp.sum(lv[0:1] * lv[1:2], axis=-1, keepdims=True))
           - jnp.exp(jnp.sum(lv[2:3] * lv[3:4], axis=-1, keepdims=True)) + lambda_init)
    n_rep = acc_ref.shape[1] // LANES
    _flash(q1_ref[...], k1_ref, v_ref, m_ref, l_ref, acc_ref, s_ref, tk)
    a1_ref[...] = acc_ref[...] / jnp.concatenate([l_ref[...]] * n_rep, axis=1)
    _flash(q2_ref[...], k2_ref, v_ref, m_ref, l_ref, acc_ref, s_ref, tk)
    o = a1_ref[...] - lam * (acc_ref[...] / jnp.concatenate([l_ref[...]] * n_rep, axis=1))
    o = o * lax.rsqrt(jnp.mean(o * o, axis=-1, keepdims=True) + RMS_EPS) * g_ref[...]
    o_ref[...] = (o * (1.0 - lambda_init)).astype(o_ref.dtype)


def _gqa_attn_kernel(q_ref, k_ref, v_ref, o_ref, vaug_ref, qs_ref, m_ref, acc_ref, s_ref, *, tk, rep):
    tq = q_ref.shape[0]

    @pl.when(pl.program_id(1) == 0)
    def _():
        _ones_behind(v_ref, vaug_ref)

    for r in range(rep):
        qs_ref[r * tq:(r + 1) * tq, :] = q_ref[:, r * LANES:(r + 1) * LANES]
    _flash(qs_ref[...], k_ref, vaug_ref, m_ref, None, acc_ref, s_ref, tk)
    for r in range(rep):
        rows = slice(r * tq, (r + 1) * tq)
        o_ref[:, r * LANES:(r + 1) * LANES] = (acc_ref[rows, :LANES] / acc_ref[rows, LANES:]).astype(o_ref.dtype)


def _mla_attn_kernel(q_ref, kn_ref, kr_ref, v_ref, o_ref, kcat_ref, vaug_ref, m_ref, acc_ref, s_ref, *, tk):
    @pl.when(pl.program_id(1) == 0)
    def _():
        kcat_ref[:, :LANES] = kn_ref[...]
        kcat_ref[:, LANES:] = kr_ref[...]
        _ones_behind(v_ref, vaug_ref)

    _flash(q_ref[...], kcat_ref, vaug_ref, m_ref, None, acc_ref, s_ref, tk)
    o_ref[...] = (acc_ref[:, :LANES] / acc_ref[:, LANES:]).astype(o_ref.dtype)


def _diff_attention(qa, ka, va, lam_vecs, subln_g, lambda_init, *, q_rows, kv_rows, tq, tk):
    n_heads = va.shape[1] // (2 * HEAD_DIM)
    q0, nq = q_rows
    k0, nk = kv_rows
    qb, kb = q0 // tq, k0 // nk
    dv = 2 * HEAD_DIM
    return pl.pallas_call(
        functools.partial(_diff_attn_kernel, tk=tk, lambda_init=lambda_init),
        grid=(n_heads, nq // tq),
        in_specs=[
            pl.BlockSpec((4, HEAD_DIM), lambda h, i: (0, 0)),
            pl.BlockSpec((1, dv), lambda h, i: (0, 0)),
            pl.BlockSpec((tq, HEAD_DIM), lambda h, i: (qb + i, 2 * h)),
            pl.BlockSpec((tq, HEAD_DIM), lambda h, i: (qb + i, 2 * h + 1)),
            pl.BlockSpec((nk, HEAD_DIM), lambda h, i: (kb, 2 * h)),
            pl.BlockSpec((nk, HEAD_DIM), lambda h, i: (kb, 2 * h + 1)),
            pl.BlockSpec((nk, dv), lambda h, i: (kb, h)),
        ],
        out_specs=pl.BlockSpec((tq, dv), lambda h, i: (i, h)),
        out_shape=jax.ShapeDtypeStruct((nq, n_heads * dv), BF16),
        scratch_shapes=[pltpu.VMEM((tq, LANES), F32), pltpu.VMEM((tq, LANES), F32), pltpu.VMEM((tq, dv), F32),
                        pltpu.VMEM((tq, tk), F32), pltpu.VMEM((tq, dv), F32)],
        compiler_params=_cparams("arbitrary", "arbitrary"),
        name="diff_attention",
    )(lam_vecs, subln_g.reshape(1, dv), qa, qa, ka, ka, va)


def _gqa_attention(qg, kg, vg, *, q_rows, kv_rows, tq, tk):
    n_kv = kg.shape[1] // HEAD_DIM
    rep = qg.shape[1] // kg.shape[1]
    q0, nq = q_rows
    k0, nk = kv_rows
    qb, kb = q0 // tq, k0 // nk
    rows = rep * tq
    return pl.pallas_call(
        functools.partial(_gqa_attn_kernel, tk=tk, rep=rep),
        grid=(n_kv, nq // tq),
        in_specs=[
            pl.BlockSpec((tq, rep * HEAD_DIM), lambda g, i: (qb + i, g)),
            pl.BlockSpec((nk, HEAD_DIM), lambda g, i: (kb, g)),
            pl.BlockSpec((nk, HEAD_DIM), lambda g, i: (kb, g)),
        ],
        out_specs=pl.BlockSpec((tq, rep * HEAD_DIM), lambda g, i: (i, g)),
        out_shape=jax.ShapeDtypeStruct((nq, qg.shape[1]), BF16),
        scratch_shapes=[pltpu.VMEM((nk, 2 * LANES), BF16), pltpu.VMEM((rows, HEAD_DIM), BF16),
                        pltpu.VMEM((rows, LANES), F32), pltpu.VMEM((rows, 2 * LANES), F32),
                        pltpu.VMEM((rows, tk), F32)],
        compiler_params=_cparams("arbitrary", "arbitrary"),
        name="gqa_attention",
    )(qg, kg, vg)


def _mla_attention(q, kv, kr, *, q_rows, kv_rows, tq, tk):
    n_heads = q.shape[1] // MLA_HEAD_PAD
    q0, nq = q_rows
    k0, nk = kv_rows
    qb, kb = q0 // tq, k0 // nk
    return pl.pallas_call(
        functools.partial(_mla_attn_kernel, tk=tk),
        grid=(n_heads, nq // tq),
        in_specs=[
            pl.BlockSpec((tq, MLA_HEAD_PAD), lambda h, i: (qb + i, h)),
            pl.BlockSpec((nk, MLA_NOPE), lambda h, i: (kb, 2 * h)),
            pl.BlockSpec((nk, LANES), lambda h, i: (kb, 0)),
            pl.BlockSpec((nk, MLA_V), lambda h, i: (kb, 2 * h + 1)),
        ],
        out_specs=pl.BlockSpec((tq, MLA_V), lambda h, i: (i, h)),
        out_shape=jax.ShapeDtypeStruct((nq, n_heads * MLA_V), BF16),
        scratch_shapes=[pltpu.VMEM((nk, MLA_HEAD_PAD), BF16), pltpu.VMEM((nk, 2 * LANES), BF16),
                        pltpu.VMEM((tq, LANES), F32), pltpu.VMEM((tq, 2 * LANES), F32),
                        pltpu.VMEM((tq, tk), F32)],
        compiler_params=_cparams("arbitrary", "arbitrary"),
        name="mla_attention",
    )(q, kv, kr, kv)


def _router_kernel(h_ref, w_ref, b_ref, g_ref):
    logits = jnp.dot(h_ref[...], w_ref[...], preferred_element_type=F32) + b_ref[...]
    lane = lax.broadcasted_iota(jnp.int32, logits.shape, 1).astype(F32)
    neg = jnp.float32(-jnp.inf)
    big = jnp.float32(LANES)

    gmask = (lane >= N_EXPERTS) & (lane < N_EXPERTS + N_GROUPS)
    gl = jnp.where(gmask, logits, neg)
    ge = jnp.exp(gl - jnp.max(gl, axis=-1, keepdims=True))
    gprob = ge / jnp.sum(ge, axis=-1, keepdims=True)
    group_w = jnp.max(gprob, axis=-1, keepdims=True)
    gidx = jnp.min(jnp.where(gmask & (gprob == group_w), lane, big), axis=-1, keepdims=True) - N_EXPERTS

    emask = (lane >= gidx * EXPERTS_PER_GROUP) & (lane < (gidx + 1) * EXPERTS_PER_GROUP)
    el = jnp.where(emask, logits, neg)
    ee = jnp.exp(el - jnp.max(el, axis=-1, keepdims=True))
    ep = ee / jnp.sum(ee, axis=-1, keepdims=True)
    p1 = jnp.max(ep, axis=-1, keepdims=True)
    i1 = jnp.min(jnp.where(emask & (ep == p1), lane, big), axis=-1, keepdims=True)
    rest = emask & (lane != i1)
    p2 = jnp.max(jnp.where(rest, ep, neg), axis=-1, keepdims=True)
    i2 = jnp.min(jnp.where(rest & (ep == p2), lane, big), axis=-1, keepdims=True)
    denom = p1 + p2
    g_ref[...] = (jnp.where(lane == i1, group_w * (p1 / denom), 0.0)
                  + jnp.where(lane == i2, group_w * (p2 / denom), 0.0)
                  + jnp.where(lane == N_EXPERTS, gidx, 0.0))


def _router(h2, w_rt, b_rt):
    m, d = h2.shape
    return pl.pallas_call(
        _router_kernel,
        grid=(m // ROW_TILE,),
        in_specs=[pl.BlockSpec((ROW_TILE, d), lambda i: (i, 0)),
                  pl.BlockSpec((d, LANES), lambda i: (0, 0)),
                  pl.BlockSpec((1, LANES), lambda i: (0, 0))],
        out_specs=pl.BlockSpec((ROW_TILE, LANES), lambda i: (i, 0)),
        out_shape=jax.ShapeDtypeStruct((m, LANES), F32),
        compiler_params=_cparams("arbitrary"),
        name="moe_router",
    )(h2, w_rt, b_rt)


def _routing_plan(group, tm, n_tiles):
    onehot = (group[:, None] == jnp.arange(N_GROUPS, dtype=jnp.int32)[None, :]).astype(jnp.int32)
    padded = (jnp.sum(onehot, axis=0) + tm - 1) // tm * tm
    ends = jnp.cumsum(padded)
    rank = jnp.take_along_axis(jnp.cumsum(onehot, axis=0), group[:, None], axis=1)[:, 0] - 1
    dest = (ends - padded)[group] + rank
    tile_group = jnp.searchsorted(ends, jnp.arange(n_tiles, dtype=jnp.int32) * tm, side="right")
    tile_group = jnp.minimum(tile_group, N_GROUPS - 1)
    return dest.astype(jnp.int32), tile_group.astype(jnp.int32), (ends[-1:] // tm).astype(jnp.int32)


def _row_copy_kernel(si_ref, di_ref, *refs, n_rows, n_streams):
    srcs = refs[:n_streams]
    dsts = refs[2 * n_streams:3 * n_streams]
    sems = refs[3 * n_streams]
    n_steps = n_rows // ROW_COPY_BLOCK

    def copies(n):
        return [pltpu.make_async_copy(srcs[j].at[pl.ds(si_ref[n], 1)], dsts[j].at[pl.ds(di_ref[n], 1)],
                                      sems.at[j]) for j in range(n_streams)]

    def wait_block():
        for _ in range(ROW_COPY_BLOCK):
            for cp in copies(0):
                cp.wait()

    def body(i, carry):
        for u in range(ROW_COPY_BLOCK):
            for cp in copies(i * ROW_COPY_BLOCK + u):
                cp.start()

        @pl.when(i >= ROW_COPY_LAG)
        def _():
            wait_block()
        return carry

    lax.fori_loop(0, n_steps, body, 0)
    for _ in range(min(ROW_COPY_LAG, n_steps)):
        wait_block()


def _row_copy(src_idx, dst_idx, srcs, inits):
    n_rows = src_idx.shape[0]
    assert n_rows % ROW_COPY_BLOCK == 0
    k = len(srcs)
    any_spec = pl.BlockSpec(memory_space=pl.ANY)
    return pl.pallas_call(
        functools.partial(_row_copy_kernel, n_rows=n_rows, n_streams=k),
        grid_spec=pltpu.PrefetchScalarGridSpec(
            num_scalar_prefetch=2, grid=(1,),
            in_specs=[any_spec] * (2 * k), out_specs=[any_spec] * k,
            scratch_shapes=[pltpu.SemaphoreType.DMA((k,))]),
        out_shape=[jax.ShapeDtypeStruct(a.shape, a.dtype) for a in inits],
        input_output_aliases={2 + k + j: j for j in range(k)},
        compiler_params=pltpu.CompilerParams(dimension_semantics=("arbitrary",)),
        name="row_copy",
    )(src_idx, dst_idx, *srcs, *inits)


def _group_up_kernel(tg_ref, nt_ref, x_ref, wg_ref, wu_ref, gates_ref, o_ref, *, f):
    nh, t = pl.program_id(0), pl.program_id(1)

    @pl.when(t < nt_ref[0])
    def _():
        x = x_ref[...]
        g = jnp.dot(x, wg_ref[...], preferred_element_type=F32)
        u = jnp.dot(x, wu_ref[...], preferred_element_type=F32)
        hid = g * jax.nn.sigmoid(g) * u
        gates = gates_ref[...]
        lane = lax.broadcasted_iota(jnp.int32, gates.shape, 1)
        n_local = hid.shape[1] // f
        for e in range(n_local):
            expert = tg_ref[t] * EXPERTS_PER_GROUP + nh * n_local + e
            gate = jnp.sum(jnp.where(lane == expert, gates, 0.0), axis=-1, keepdims=True)
            o_ref[:, e * f:(e + 1) * f] = (hid[:, e * f:(e + 1) * f] * gate).astype(o_ref.dtype)


def _group_up(tile_group, n_used, xs, w_gate_g, w_up_g, gates_s, f):
    mp, d = xs.shape
    n_wide = w_gate_g.shape[2]
    tn = n_wide // 2
    tm = MOE_TM
    wspec = pl.BlockSpec((None, d, tn), lambda nh, t, tg, nt: (tg[t], 0, nh))
    return pl.pallas_call(
        functools.partial(_group_up_kernel, f=f),
        grid_spec=pltpu.PrefetchScalarGridSpec(
            num_scalar_prefetch=2, grid=(n_wide // tn, mp // tm),
            in_specs=[pl.BlockSpec((tm, d), lambda nh, t, tg, nt: (t, 0)), wspec, wspec,
                      pl.BlockSpec((tm, LANES), lambda nh, t, tg, nt: (t, 0))],
            out_specs=pl.BlockSpec((tm, tn), lambda nh, t, tg, nt: (t, nh))),
        out_shape=jax.ShapeDtypeStruct((mp, n_wide), BF16),
        compiler_params=_cparams("arbitrary", "arbitrary"),
        name="moe_group_up",
    )(tile_group, n_used, xs, w_gate_g, w_up_g, gates_s)


def _group_down_kernel(tg_ref, nt_ref, h_ref, w_ref, o_ref):
    @pl.when(pl.program_id(0) < nt_ref[0])
    def _():
        o_ref[...] = jnp.dot(h_ref[...], w_ref[...], preferred_element_type=F32)


def _group_down(tile_group, n_used, hidden_s, w_down_g):
    mp, n_wide = hidden_s.shape
    d = w_down_g.shape[2]
    tm = MOE_TM
    return pl.pallas_call(
        _group_down_kernel,
        grid_spec=pltpu.PrefetchScalarGridSpec(
            num_scalar_prefetch=2, grid=(mp // tm,),
            in_specs=[pl.BlockSpec((tm, n_wide), lambda t, tg, nt: (t, 0)),
                      pl.BlockSpec((None, n_wide, d), lambda t, tg, nt: (tg[t], 0, 0))],
            out_specs=pl.BlockSpec((tm, d), lambda t, tg, nt: (t, 0))),
        out_shape=jax.ShapeDtypeStruct((mp, d), F32),
        compiler_params=_cparams("arbitrary"),
        name="moe_group_down",
    )(tile_group, n_used, hidden_s, w_down_g)


def _routed_experts(h2, gates, w_gate, w_up, w_down):
    m, d = h2.shape
    n_exp, _, f = w_gate.shape
    tm = MOE_TM
    mp = m + N_GROUPS * tm
    group = gates[:, N_EXPERTS].astype(jnp.int32)
    dest, tile_group, n_used = _routing_plan(group, tm, mp // tm)
    rows = jnp.arange(m, dtype=jnp.int32)
    slab16 = (BF16_SUBLANES, d // BF16_SUBLANES)
    slab32 = (F32_SUBLANES, d // F32_SUBLANES)
    xs, gates_s = _row_copy(rows, dest, [h2.reshape(m, *slab16), gates],
                            [jnp.zeros((mp, *slab16), BF16), jnp.zeros((mp, LANES), F32)])
    xs = xs.reshape(mp, d)

    def by_group(w):
        return (w.reshape(N_GROUPS, EXPERTS_PER_GROUP, d, f).transpose(0, 2, 1, 3)
                .reshape(N_GROUPS, d, EXPERTS_PER_GROUP * f).astype(BF16))

    hidden_s = _group_up(tile_group, n_used, xs, by_group(w_gate), by_group(w_up), gates_s, f)
    y_s = _group_down(tile_group, n_used, hidden_s,
                      w_down.reshape(N_GROUPS, EXPERTS_PER_GROUP * f, d).astype(BF16))
    (y,) = _row_copy(dest, rows, [y_s.reshape(mp, *slab32)], [jnp.zeros((m, *slab32), F32)])
    return y.reshape(m, d)


def kernel(x, c, ctx, c_ctx, w_ada, b_ada, ln1_g, ln1_b, ln2_g, ln2_b, ev_w_in, ev_w_out, diff_lambda, diff_subln_g, gqa_q_norm_g, gqa_k_norm_g, od_w_in, mla_q_norm_g, mla_kv_norm_g, mla_w_uq, mla_w_ukv, od_w_out, moe_w_group, moe_b_group, moe_w_router, moe_b_router, moe_w_gate, moe_w_up, moe_w_down):
    batch, seq, d = x.shape
    n_ctx = ctx.shape[1]
    depth = w_ada.shape[0]
    assert batch == 1 and seq % 1024 == 0 and n_ctx == ROW_TILE and (seq + n_ctx) % MM_TM == 0
    m = seq + n_ctx
    alpha = (2.0 * depth) ** 0.25
    lat_rows, ctx_rows = (0, seq), (seq, n_ctx)
    all_rows = (0, m)

    xs = jnp.concatenate([x.reshape(seq, d), ctx.reshape(n_ctx, d)], axis=0)
    mods3 = _ada_mods(c, c_ctx, w_ada, b_ada).reshape(depth * 2 * N_MOD, 1, d)
    tab_a = _rope_tables(seq, n_ctx, HEAD_DIM)
    tab_m = _rope_tables(seq, n_ctx, MLA_ROPE)

    h = _modulate(xs, mods3, 0, 1, 0, seq)
    for layer in range(depth):
        i = layer // 2
        if layer % 2 == 0:
            lambda_init = 0.8 - 0.6 * math.exp(-0.3 * layer)
            scale = LOG2E / math.sqrt(HEAD_DIM)
            w_in = ev_w_in[i]
            n_diff_qk = d // 2
            n_diff_v = d // 2
            n_gqa_q = d // 2
            n_gqa_kv = n_gqa_q // 4
            o0 = 0
            segs = []
            for width in (n_diff_qk, n_diff_qk, n_diff_v, n_gqa_q, n_gqa_kv, n_gqa_kv):
                segs.append(w_in[:, o0:o0 + width].astype(BF16))
                o0 += width
            qa = _head_proj(h, segs[0], tn=512, tables=tab_a, scale=scale, name="proj_diff_q")
            ka = _head_proj(h, segs[1], tn=512, tables=tab_a, name="proj_diff_k")
            va = _matmul(h, segs[2], BF16, tn=512, name="proj_diff_v")
            qg = _head_proj(h, segs[3], tn=512, norm_g=gqa_q_norm_g[i], tables=tab_a, scale=scale,
                            name="proj_gqa_q")
            kg = _head_proj(h, segs[4], tn=512, norm_g=gqa_k_norm_g[i], tables=tab_a, name="proj_gqa_k")
            vg = _matmul(h, segs[5], BF16, tn=512, name="proj_gqa_v")
            parts = []
            for q_rows, kv_rows, tq_d, tq_g, tk in ((lat_rows, all_rows, ATTN_ROWS, ATTN_ROWS // 4, KV_CHUNK),
                                                    (ctx_rows, ctx_rows, n_ctx, n_ctx, n_ctx)):
                od = _diff_attention(qa, ka, va, diff_lambda[i], diff_subln_g[i], lambda_init,
                                     q_rows=q_rows, kv_rows=kv_rows, tq=tq_d, tk=tk)
                og = _gqa_attention(qg, kg, vg, q_rows=q_rows, kv_rows=kv_rows, tq=tq_g, tk=tk)
                parts.append(jnp.concatenate([od, og], axis=1))
            attn = jnp.concatenate(parts, axis=0)
            w_out = ev_w_out[i].astype(BF16)
        else:
            scale = LOG2E / math.sqrt(MLA_NOPE + MLA_ROPE)
            w_in = od_w_in[i]
            q_rank = mla_q_norm_g.shape[1]
            kv_rank = mla_kv_norm_g.shape[1]
            n_heads = mla_w_uq.shape[2] // (MLA_NOPE + MLA_ROPE)
            vec = lambda n: pl.BlockSpec((1, n), lambda j, i_: (0, 0))
            tab_specs = [pl.BlockSpec((MM_TM, LANES), lambda j, i_: (i_, 0))] * 3
            cq = _matmul(h, w_in[:, :q_rank].astype(BF16), BF16, tn=q_rank, epilogue=_rmsnorm_epilogue,
                         extras=[mla_q_norm_g[i].reshape(1, q_rank)], extra_specs=[vec(q_rank)],
                         name="proj_mla_cq")
            ckv = _matmul(h, w_in[:, q_rank:q_rank + kv_rank].astype(BF16), BF16, tn=kv_rank,
                          epilogue=_rmsnorm_epilogue, extras=[mla_kv_norm_g[i].reshape(1, kv_rank)],
                          extra_specs=[vec(kv_rank)], name="proj_mla_ckv")
            w_kr = jnp.pad(w_in[:, q_rank + kv_rank:], ((0, 0), (0, LANES - MLA_ROPE))).astype(BF16)
            kr = _matmul(h, w_kr, BF16, tn=LANES, epilogue=_rope_m_epilogue, extras=tab_m,
                         extra_specs=tab_specs, name="proj_mla_krope")
            w_uq = jnp.pad(mla_w_uq[i].reshape(q_rank, n_heads, MLA_NOPE + MLA_ROPE),
                           ((0, 0), (0, 0), (0, MLA_HEAD_PAD - MLA_NOPE - MLA_ROPE)))
            w_uq = w_uq.reshape(q_rank, n_heads * MLA_HEAD_PAD).astype(BF16)
            q = _matmul(cq, w_uq, BF16, tn=1024, epilogue=functools.partial(_mla_q_epilogue, scale=scale),
                        extras=tab_m, extra_specs=tab_specs, name="proj_mla_q")
            kv = _matmul(ckv, mla_w_ukv[i].astype(BF16), BF16, tn=1024, name="proj_mla_kv")
            attn = jnp.concatenate([
                _mla_attention(q, kv, kr, q_rows=lat_rows, kv_rows=all_rows, tq=ATTN_ROWS, tk=KV_CHUNK),
                _mla_attention(q, kv, kr, q_rows=ctx_rows, kv_rows=ctx_rows, tq=n_ctx, tk=n_ctx)], axis=0)
            w_out = od_w_out[i].astype(BF16)

        o = _matmul(attn, w_out, F32, tn=1024, name="proj_out")
        xs, h2 = _resid_ln(xs, o, mods3, layer, 2, ln1_g[layer], ln1_b[layer], layer, 4, 3, seq, alpha)

        w_rt = jnp.concatenate([moe_w_router[layer], moe_w_group[layer],
                                jnp.zeros((d, LANES - N_EXPERTS - N_GROUPS), F32)], axis=1).astype(BF16)
        b_rt = jnp.concatenate([moe_b_router[layer], moe_b_group[layer],
                                jnp.zeros((LANES - N_EXPERTS - N_GROUPS,), F32)]).reshape(1, LANES)
        gates = _router(h2, w_rt, b_rt)
        y = _routed_experts(h2, gates, moe_w_gate[layer], moe_w_up[layer], moe_w_down[layer])
        nxt = min(layer + 1, depth - 1)
        xs, h = _resid_ln(xs, y, mods3, layer, 5, ln2_g[layer], ln2_b[layer], nxt, 1, 0, seq, alpha)
    return xs[:seq].reshape(batch, seq, d)
```

```python
import functools
import math

import jax
import jax.numpy as jnp
from jax import lax
from jax.experimental import pallas as pl
from jax.experimental.pallas import tpu as pltpu

F32 = jnp.float32
BF16 = jnp.bfloat16

GRID_W = 64
HEAD_DIM = 128
ROPE_THETA = 10000.0
LN_EPS = 1e-5
RMS_EPS = 1e-6
N_GROUPS = 4
EXPERTS_PER_GROUP = 4
N_EXPERTS = N_GROUPS * EXPERTS_PER_GROUP
N_MOD = 6
MLA_NOPE = 128
MLA_ROPE = 64
MLA_V = 128
MLA_HEAD_PAD = 256
LOG2E = math.log2(math.e)

LANES = 128
V7X_VMEM_LIMIT = 56 * 2**20

ROW_TILE = 256
MM_TM = 768
KV_CHUNK = 768
ATTN_ROWS = 2048
MOE_TM = 256


def _cparams(*sem):
    return pltpu.CompilerParams(dimension_semantics=sem, vmem_limit_bytes=V7X_VMEM_LIMIT)


def _ada_kernel(ct_ref, w_ref, b_ref, o_ref):
    ct = ct_ref[...]
    s = ct * jax.nn.sigmoid(ct)
    w = w_ref[...]
    b = b_ref[...]
    for r in range(2):
        o_ref[r:r + 1, :] = jnp.sum(w * s[:, r:r + 1], axis=0, keepdims=True) + b


def _ada_mods(c, c_ctx, w_ada, b_ada):
    depth, d, n = w_ada.shape
    tn = 512
    ct = jnp.stack([c.reshape(d), c_ctx.reshape(d)], axis=1)
    return pl.pallas_call(
        _ada_kernel,
        grid=(depth, n // tn),
        in_specs=[
            pl.BlockSpec((d, 2), lambda l, j: (0, 0)),
            pl.BlockSpec((None, d, tn), lambda l, j: (l, 0, j)),
            pl.BlockSpec((None, 1, tn), lambda l, j: (l, 0, j)),
        ],
        out_specs=pl.BlockSpec((None, 2, tn), lambda l, j: (l, 0, j)),
        out_shape=jax.ShapeDtypeStruct((depth, 2, n), F32),
        compiler_params=_cparams("arbitrary", "arbitrary"),
        name="ada_mods",
    )(ct, w_ada, b_ada.reshape(depth, 1, n))


def _mod_spec(layer, which, d, seq):
    def index(i):
        sel = jnp.where(i * ROW_TILE >= seq, 1, 0)
        return ((layer * 2 + sel) * N_MOD + which, 0, 0)
    return pl.BlockSpec((None, 1, d), index)


def _modulate_kernel(x_ref, sc_ref, sh_ref, h_ref):
    h_ref[...] = (x_ref[...] * (1.0 + sc_ref[...]) + sh_ref[...]).astype(h_ref.dtype)


def _modulate(x, mods3, layer, sc_idx, sh_idx, seq):
    m, d = x.shape
    return pl.pallas_call(
        _modulate_kernel,
        grid=(m // ROW_TILE,),
        in_specs=[pl.BlockSpec((ROW_TILE, d), lambda i: (i, 0)),
                  _mod_spec(layer, sc_idx, d, seq), _mod_spec(layer, sh_idx, d, seq)],
        out_specs=pl.BlockSpec((ROW_TILE, d), lambda i: (i, 0)),
        out_shape=jax.ShapeDtypeStruct((m, d), BF16),
        compiler_params=_cparams("arbitrary"),
        name="modulate",
    )(x, mods3, mods3)


def _resid_ln_kernel(x_ref, o_ref, g_ref, lg_ref, lb_ref, sc_ref, sh_ref, y_ref, h_ref, *, alpha):
    z = alpha * x_ref[...] + g_ref[...] * o_ref[...]
    mu = jnp.mean(z, axis=-1, keepdims=True)
    zc = z - mu
    var = jnp.mean(zc * zc, axis=-1, keepdims=True)
    y = zc * lax.rsqrt(var + LN_EPS) * lg_ref[...] + lb_ref[...]
    y_ref[...] = y
    h_ref[...] = (y * (1.0 + sc_ref[...]) + sh_ref[...]).astype(h_ref.dtype)


def _resid_ln(x, o, mods3, layer, gate_idx, ln_g, ln_b, nxt_layer, sc_idx, sh_idx, seq, alpha, h_dtype=BF16):
    m, d = x.shape
    row = pl.BlockSpec((ROW_TILE, d), lambda i: (i, 0))
    vec = pl.BlockSpec((1, d), lambda i: (0, 0))
    return pl.pallas_call(
        functools.partial(_resid_ln_kernel, alpha=alpha),
        grid=(m // ROW_TILE,),
        in_specs=[row, row, _mod_spec(layer, gate_idx, d, seq), vec, vec,
                  _mod_spec(nxt_layer, sc_idx, d, seq), _mod_spec(nxt_layer, sh_idx, d, seq)],
        out_specs=[row, row],
        out_shape=[jax.ShapeDtypeStruct((m, d), F32), jax.ShapeDtypeStruct((m, d), h_dtype)],
        compiler_params=_cparams("arbitrary"),
        name="resid_ln",
    )(x, o, mods3, ln_g.reshape(1, d), ln_b.reshape(1, d), mods3, mods3)


def _mm_kernel(*refs, epilogue, n_extra):
    a_ref, b_ref = refs[0], refs[1]
    extras = refs[2:2 + n_extra]
    o_ref = refs[2 + n_extra]
    acc = jnp.dot(a_ref[...], b_ref[...], preferred_element_type=F32)
    epilogue(acc, o_ref, *extras)


def _store(acc, o_ref):
    o_ref[...] = acc.astype(o_ref.dtype)


def _matmul(a, b, out_dtype, *, tn, epilogue=_store, extras=(), extra_specs=(), name="matmul"):
    m, k = a.shape
    n = b.shape[1]
    tm = MM_TM
    tn = min(tn, n)
    assert m % tm == 0 and n % tn == 0
    return pl.pallas_call(
        functools.partial(_mm_kernel, epilogue=epilogue, n_extra=len(extras)),
        grid=(n // tn, m // tm),
        in_specs=[pl.BlockSpec((tm, k), lambda j, i: (i, 0)),
                  pl.BlockSpec((k, tn), lambda j, i: (0, j)),
                  *extra_specs],
        out_specs=pl.BlockSpec((tm, tn), lambda j, i: (i, j)),
        out_shape=jax.ShapeDtypeStruct((m, n), out_dtype),
        compiler_params=_cparams("arbitrary", "arbitrary"),
        name=name,
    )(a, b, *extras)


def _rope_tables(seq, n_ctx, rot_dim):
    rows = seq // GRID_W
    row, col = jnp.meshgrid(jnp.arange(rows), jnp.arange(GRID_W), indexing="ij")
    row = row.reshape(-1).astype(F32)
    col = col.reshape(-1).astype(F32)
    quarter = rot_dim // 4
    inv_freq = ROPE_THETA ** (-jnp.arange(quarter, dtype=F32) / quarter)
    ang = jnp.concatenate([row[:, None] * inv_freq, col[:, None] * inv_freq], axis=-1)
    cos, sin = jnp.cos(ang), jnp.sin(ang)
    half = rot_dim // 2
    pad = LANES - rot_dim
    zeros_h = jnp.zeros((seq, half), F32)
    zeros_p = jnp.zeros((seq, pad), F32)
    cos_t = jnp.concatenate([cos, cos, zeros_p], axis=-1)
    sin_lo = jnp.concatenate([-sin, zeros_h, zeros_p], axis=-1)
    sin_hi = jnp.concatenate([zeros_h, sin, zeros_p], axis=-1)
    ident = jnp.concatenate([jnp.ones((n_ctx, rot_dim), F32), jnp.zeros((n_ctx, pad), F32)], axis=-1)
    zc = jnp.zeros((n_ctx, LANES), F32)
    return (jnp.concatenate([cos_t, ident], axis=0), jnp.concatenate([sin_lo, zc], axis=0),
            jnp.concatenate([sin_hi, zc], axis=0))


def _rope_block(x, cos, sin_lo, sin_hi, half):
    return (x * cos + pltpu.roll(x, LANES - half, axis=1) * sin_lo
            + pltpu.roll(x, half, axis=1) * sin_hi)


def _head_epilogue(acc, o_ref, *extras, norm, rope, scale):
    idx = 0
    if norm:
        g = extras[idx][...]
        idx += 1
    if rope:
        cos, sin_lo, sin_hi = (r[...] for r in extras[idx:idx + 3])
    for hb in range(acc.shape[1] // LANES):
        x = acc[:, hb * LANES:(hb + 1) * LANES]
        if norm:
            x = x * lax.rsqrt(jnp.mean(x * x, axis=-1, keepdims=True) + RMS_EPS) * g
        if rope:
            x = _rope_block(x, cos, sin_lo, sin_hi, HEAD_DIM // 2)
        if scale != 1.0:
            x = x * scale
        o_ref[:, hb * LANES:(hb + 1) * LANES] = x.astype(o_ref.dtype)


def _head_proj(h, w, *, tn, norm_g=None, tables=None, scale=1.0, name):
    extras, specs = [], []
    if norm_g is not None:
        extras.append(norm_g.reshape(1, LANES))
        specs.append(pl.BlockSpec((1, LANES), lambda j, i: (0, 0)))
    if tables is not None:
        extras.extend(tables)
        specs.extend([pl.BlockSpec((MM_TM, LANES), lambda j, i: (i, 0))] * 3)
    ep = functools.partial(_head_epilogue, norm=norm_g is not None, rope=tables is not None, scale=scale)
    return _matmul(h, w, BF16, tn=tn, epilogue=ep, extras=extras, extra_specs=specs, name=name)


def _rmsnorm_epilogue(acc, o_ref, g_ref):
    x = acc * lax.rsqrt(jnp.mean(acc * acc, axis=-1, keepdims=True) + RMS_EPS) * g_ref[...]
    o_ref[...] = x.astype(o_ref.dtype)


def _rope_m_epilogue(acc, o_ref, cos_ref, lo_ref, hi_ref):
    o_ref[...] = _rope_block(acc, cos_ref[...], lo_ref[...], hi_ref[...], MLA_ROPE // 2).astype(o_ref.dtype)


def _mla_q_epilogue(acc, o_ref, cos_ref, lo_ref, hi_ref, *, scale):
    cos, lo, hi = cos_ref[...], lo_ref[...], hi_ref[...]
    for hd in range(acc.shape[1] // MLA_HEAD_PAD):
        base = hd * MLA_HEAD_PAD
        o_ref[:, base:base + LANES] = (acc[:, base:base + LANES] * scale).astype(o_ref.dtype)
        r = _rope_block(acc[:, base + LANES:base + 2 * LANES], cos, lo, hi, MLA_ROPE // 2)
        o_ref[:, base + LANES:base + 2 * LANES] = (r * scale).astype(o_ref.dtype)


def _softmax_step(s, v, m_ref, l_ref, acc_ref):
    m_old = m_ref[...]
    m_new = jnp.maximum(m_old, jnp.max(s, axis=-1, keepdims=True))
    p = jnp.concatenate([jnp.exp2(s[:, j * LANES:(j + 1) * LANES] - m_new)
                         for j in range(s.shape[1] // LANES)], axis=1)
    alpha = jnp.exp2(m_old - m_new)
    if l_ref is not None:
        l_ref[...] = alpha * l_ref[...] + jnp.sum(p, axis=-1, keepdims=True)
    pv = jnp.dot(p.astype(BF16), v, preferred_element_type=F32)
    acc_ref[...] = jnp.concatenate([alpha] * (acc_ref.shape[1] // LANES), axis=1) * acc_ref[...] + pv
    m_ref[...] = m_new


def _flash(q, k_ref, v_ref, m_ref, l_ref, acc_ref, s_ref, tk):
    n_chunks = k_ref.shape[0] // tk
    m_ref[...] = jnp.full(m_ref.shape, -jnp.inf, F32)
    acc_ref[...] = jnp.zeros(acc_ref.shape, F32)
    if l_ref is not None:
        l_ref[...] = jnp.zeros(l_ref.shape, F32)
    nt_dims = (((1,), (1,)), ((), ()))
    s_ref[...] = lax.dot_general(q, k_ref[pl.ds(0, tk), :], nt_dims, preferred_element_type=F32)

    def body(c, carry):
        start = pl.multiple_of(c * tk, tk)
        nxt = pl.multiple_of((c + 1) * tk, tk)
        s = s_ref[...]
        s_next = lax.dot_general(q, k_ref[pl.ds(nxt, tk), :], nt_dims, preferred_element_type=F32)
        _softmax_step(s, v_ref[pl.ds(start, tk), :], m_ref, l_ref, acc_ref)
        s_ref[...] = s_next
        return carry

    lax.fori_loop(0, n_chunks - 1, body, 0)
    _softmax_step(s_ref[...], v_ref[pl.ds((n_chunks - 1) * tk, tk), :], m_ref, l_ref, acc_ref)


def _ones_behind(v_ref, vaug_ref):
    vaug_ref[:, :LANES] = v_ref[...]
    vaug_ref[:, LANES:] = jnp.ones((v_ref.shape[0], LANES), BF16)


def _diff_attn_kernel(lam_ref, g_ref, q1_ref, q2_ref, k1_ref, k2_ref, v_ref, o_ref,
                      m_ref, l_ref, acc_ref, s_ref, a1_ref, *, tk, lambda_init):
    lv = lam_ref[...]
    lam = (jnp.exp(jnp.sum(lv[0:1] * lv[1:2], axis=-1, keepdims=True))
           - jnp.exp(jnp.sum(lv[2:3] * lv[3:4], axis=-1, keepdims=True)) + lambda_init)
    n_rep = acc_ref.shape[1] // LANES
    _flash(q1_ref[...], k1_ref, v_ref, m_ref, l_ref, acc_ref, s_ref, tk)
    a1_ref[...] = acc_ref[...] / jnp.concatenate([l_ref[...]] * n_rep, axis=1)
    _flash(q2_ref[...], k2_ref, v_ref, m_ref, l_ref, acc_ref, s_ref, tk)
    o = a1_ref[...] - lam * (acc_ref[...] / jnp.concatenate([l_ref[...]] * n_rep, axis=1))
    o = o * lax.rsqrt(jnp.mean(o * o, axis=-1, keepdims=True) + RMS_EPS) * g_ref[...]
    o_ref[...] = (o * (1.0 - lambda_init)).astype(o_ref.dtype)


def _gqa_attn_kernel(q_ref, k_ref, v_ref, o_ref, vaug_ref, qs_ref, m_ref, acc_ref, s_ref, *, tk, rep):
    tq = q_ref.shape[0]

    @pl.when(pl.program_id(1) == 0)
    def _():
        _ones_behind(v_ref, vaug_ref)

    for r in range(rep):
        qs_ref[r * tq:(r + 1) * tq, :] = q_ref[:, r * LANES:(r + 1) * LANES]
    _flash(qs_ref[...], k_ref, vaug_ref, m_ref, None, acc_ref, s_ref, tk)
    for r in range(rep):
        rows = slice(r * tq, (r + 1) * tq)
        o_ref[:, r * LANES:(r + 1) * LANES] = (acc_ref[rows, :LANES] / acc_ref[rows, LANES:]).astype(o_ref.dtype)


def _mla_attn_kernel(q_ref, kn_ref, kr_ref, v_ref, o_ref, kcat_ref, vaug_ref, m_ref, acc_ref, s_ref, *, tk):
    @pl.when(pl.program_id(1) == 0)
    def _():
        kcat_ref[:, :LANES] = kn_ref[...]
        kcat_ref[:, LANES:] = kr_ref[...]
        _ones_behind(v_ref, vaug_ref)

    _flash(q_ref[...], kcat_ref, vaug_ref, m_ref, None, acc_ref, s_ref, tk)
    o_ref[...] = (acc_ref[:, :LANES] / acc_ref[:, LANES:]).astype(o_ref.dtype)


def _diff_attention(qa, ka, va, lam_vecs, subln_g, lambda_init, *, q_rows, kv_rows, tq, tk):
    n_heads = va.shape[1] // (2 * HEAD_DIM)
    q0, nq = q_rows
    k0, nk = kv_rows
    qb, kb = q0 // tq, k0 // nk
    dv = 2 * HEAD_DIM
    return pl.pallas_call(
        functools.partial(_diff_attn_kernel, tk=tk, lambda_init=lambda_init),
        grid=(n_heads, nq // tq),
        in_specs=[
            pl.BlockSpec((4, HEAD_DIM), lambda h, i: (0, 0)),
            pl.BlockSpec((1, dv), lambda h, i: (0, 0)),
            pl.BlockSpec((tq, HEAD_DIM), lambda h, i: (qb + i, 2 * h)),
            pl.BlockSpec((tq, HEAD_DIM), lambda h, i: (qb + i, 2 * h + 1)),
            pl.BlockSpec((nk, HEAD_DIM), lambda h, i: (kb, 2 * h)),
            pl.BlockSpec((nk, HEAD_DIM), lambda h, i: (kb, 2 * h + 1)),
            pl.BlockSpec((nk, dv), lambda h, i: (kb, h)),
        ],
        out_specs=pl.BlockSpec((tq, dv), lambda h, i: (i, h)),
        out_shape=jax.ShapeDtypeStruct((nq, n_heads * dv), BF16),
        scratch_shapes=[pltpu.VMEM((tq, LANES), F32), pltpu.VMEM((tq, LANES), F32), pltpu.VMEM((tq, dv), F32),
                        pltpu.VMEM((tq, tk), F32), pltpu.VMEM((tq, dv), F32)],
        compiler_params=_cparams("arbitrary", "arbitrary"),
        name="diff_attention",
    )(lam_vecs, subln_g.reshape(1, dv), qa, qa, ka, ka, va)


def _gqa_attention(qg, kg, vg, *, q_rows, kv_rows, tq, tk):
    n_kv = kg.shape[1] // HEAD_DIM
    rep = qg.shape[1] // kg.shape[1]
    q0, nq = q_rows
    k0, nk = kv_rows
    qb, kb = q0 // tq, k0 // nk
    rows = rep * tq
    return pl.pallas_call(
        functools.partial(_gqa_attn_kernel, tk=tk, rep=rep),
        grid=(n_kv, nq // tq),
        in_specs=[
            pl.BlockSpec((tq, rep * HEAD_DIM), lambda g, i: (qb + i, g)),
            pl.BlockSpec((nk, HEAD_DIM), lambda g, i: (kb, g)),
            pl.BlockSpec((nk, HEAD_DIM), lambda g, i: (kb, g)),
        ],
        out_specs=pl.BlockSpec((tq, rep * HEAD_DIM), lambda g, i: (i, g)),
        out_shape=jax.ShapeDtypeStruct((nq, qg.shape[1]), BF16),
        scratch_shapes=[pltpu.VMEM((nk, 2 * LANES), BF16), pltpu.VMEM((rows, HEAD_DIM), BF16),
                        pltpu.VMEM((rows, LANES), F32), pltpu.VMEM((rows, 2 * LANES), F32),
                        pltpu.VMEM((rows, tk), F32)],
        compiler_params=_cparams("arbitrary", "arbitrary"),
        name="gqa_attention",
    )(qg, kg, vg)


def _mla_attention(q, kv, kr, *, q_rows, kv_rows, tq, tk):
    n_heads = q.shape[1] // MLA_HEAD_PAD
    q0, nq = q_rows
    k0, nk = kv_rows
    qb, kb = q0 // tq, k0 // nk
    return pl.pallas_call(
        functools.partial(_mla_attn_kernel, tk=tk),
        grid=(n_heads, nq // tq),
        in_specs=[
            pl.BlockSpec((tq, MLA_HEAD_PAD), lambda h, i: (qb + i, h)),
            pl.BlockSpec((nk, MLA_NOPE), lambda h, i: (kb, 2 * h)),
            pl.BlockSpec((nk, LANES), lambda h, i: (kb, 0)),
            pl.BlockSpec((nk, MLA_V), lambda h, i: (kb, 2 * h + 1)),
        ],
        out_specs=pl.BlockSpec((tq, MLA_V), lambda h, i: (i, h)),
        out_shape=jax.ShapeDtypeStruct((nq, n_heads * MLA_V), BF16),
        scratch_shapes=[pltpu.VMEM((nk, MLA_HEAD_PAD), BF16), pltpu.VMEM((nk, 2 * LANES), BF16),
                        pltpu.VMEM((tq, LANES), F32), pltpu.VMEM((tq, 2 * LANES), F32),
                        pltpu.VMEM((tq, tk), F32)],
        compiler_params=_cparams("arbitrary", "arbitrary"),
        name="mla_attention",
    )(q, kv, kr, kv)


def _router_kernel(h_ref, w_ref, b_ref, g_ref):
    logits = jnp.dot(h_ref[...].astype(BF16), w_ref[...], preferred_element_type=F32) + b_ref[...]
    lane = lax.broadcasted_iota(jnp.int32, logits.shape, 1).astype(F32)
    neg = jnp.float32(-jnp.inf)
    big = jnp.float32(LANES)

    gmask = (lane >= N_EXPERTS) & (lane < N_EXPERTS + N_GROUPS)
    gl = jnp.where(gmask, logits, neg)
    ge = jnp.exp(gl - jnp.max(gl, axis=-1, keepdims=True))
    gprob = ge / jnp.sum(ge, axis=-1, keepdims=True)
    group_w = jnp.max(gprob, axis=-1, keepdims=True)
    gidx = jnp.min(jnp.where(gmask & (gprob == group_w), lane, big), axis=-1, keepdims=True) - N_EXPERTS

    emask = (lane >= gidx * EXPERTS_PER_GROUP) & (lane < (gidx + 1) * EXPERTS_PER_GROUP)
    el = jnp.where(emask, logits, neg)
    ee = jnp.exp(el - jnp.max(el, axis=-1, keepdims=True))
    ep = ee / jnp.sum(ee, axis=-1, keepdims=True)
    p1 = jnp.max(ep, axis=-1, keepdims=True)
    i1 = jnp.min(jnp.where(emask & (ep == p1), lane, big), axis=-1, keepdims=True)
    rest = emask & (lane != i1)
    p2 = jnp.max(jnp.where(rest, ep, neg), axis=-1, keepdims=True)
    i2 = jnp.min(jnp.where(rest & (ep == p2), lane, big), axis=-1, keepdims=True)
    denom = p1 + p2
    g_ref[...] = (jnp.where(lane == i1, group_w * (p1 / denom), 0.0)
                  + jnp.where(lane == i2, group_w * (p2 / denom), 0.0)
                  + jnp.where(lane == N_EXPERTS, gidx, 0.0))


def _router(h2, w_rt, b_rt):
    m, d = h2.shape
    return pl.pallas_call(
        _router_kernel,
        grid=(m // ROW_TILE,),
        in_specs=[pl.BlockSpec((ROW_TILE, d), lambda i: (i, 0)),
                  pl.BlockSpec((d, LANES), lambda i: (0, 0)),
                  pl.BlockSpec((1, LANES), lambda i: (0, 0))],
        out_specs=pl.BlockSpec((ROW_TILE, LANES), lambda i: (i, 0)),
        out_shape=jax.ShapeDtypeStruct((m, LANES), F32),
        compiler_params=_cparams("arbitrary"),
        name="moe_router",
    )(h2, w_rt, b_rt)


def _routing_plan(group, tm, n_tiles):
    m = group.shape[0]
    onehot = (group[:, None] == jnp.arange(N_GROUPS, dtype=jnp.int32)[None, :]).astype(jnp.int32)
    counts = jnp.sum(onehot, axis=0)
    padded = (counts + tm - 1) // tm * tm
    ends = jnp.cumsum(padded)
    starts = ends - padded
    rank = jnp.take_along_axis(jnp.cumsum(onehot, axis=0), group[:, None], axis=1)[:, 0] - 1
    slot = starts[group] + rank
    slot_token = jnp.zeros((n_tiles * tm,), jnp.int32).at[slot].set(jnp.arange(m, dtype=jnp.int32))
    tile_start = jnp.arange(n_tiles, dtype=jnp.int32) * tm
    tile_group = jnp.minimum(jnp.searchsorted(ends, tile_start, side="right"), N_GROUPS - 1)
    tile_rows = jnp.clip((starts + counts)[tile_group] - tile_start, 0, tm)
    return slot_token, tile_group.astype(jnp.int32), tile_rows.astype(jnp.int32)


def _gather_rows_kernel(tok_ref, nrow_ref, x_ref, g_ref, xo_ref, go_ref, buf_ref, sems):
    t = pl.program_id(0)
    n_rows = nrow_ref[t]
    tm = buf_ref.shape[0]

    @pl.when(t == 0)
    def _():
        buf_ref[...] = jnp.zeros(buf_ref.shape, buf_ref.dtype)

    go_ref[...] = jnp.zeros(go_ref.shape, go_ref.dtype)

    def copies(r, tok):
        return (pltpu.make_async_copy(x_ref.at[pl.ds(tok, 1)], buf_ref.at[pl.ds(r, 1)], sems.at[0]),
                pltpu.make_async_copy(g_ref.at[pl.ds(tok, 1)], go_ref.at[pl.ds(r, 1)], sems.at[1]))

    def issue(r, carry):
        for cp in copies(r, tok_ref[t * tm + r]):
            cp.start()
        return carry

    def drain(r, carry):
        for cp in copies(0, 0):
            cp.wait()
        return carry

    lax.fori_loop(0, n_rows, issue, 0)
    lax.fori_loop(0, n_rows, drain, 0)
    xo_ref[...] = buf_ref[...].astype(xo_ref.dtype)


def _gather_rows(slot_token, tile_rows, x, gates):
    n_slots = slot_token.shape[0]
    d = x.shape[1]
    tm = MOE_TM
    any_spec = pl.BlockSpec(memory_space=pl.ANY)
    return pl.pallas_call(
        _gather_rows_kernel,
        grid_spec=pltpu.PrefetchScalarGridSpec(
            num_scalar_prefetch=2, grid=(n_slots // tm,),
            in_specs=[any_spec, any_spec],
            out_specs=[pl.BlockSpec((tm, d), lambda t, tok, nr: (t, 0)),
                       pl.BlockSpec((tm, LANES), lambda t, tok, nr: (t, 0))],
            scratch_shapes=[pltpu.VMEM((tm, d), x.dtype), pltpu.SemaphoreType.DMA((2,))]),
        out_shape=[jax.ShapeDtypeStruct((n_slots, d), BF16), jax.ShapeDtypeStruct((n_slots, LANES), F32)],
        compiler_params=_cparams("arbitrary"),
        name="moe_gather_rows",
    )(slot_token, tile_rows, x, gates)


def _group_up_kernel(tg_ref, nrow_ref, x_ref, wg_ref, wu_ref, gates_ref, o_ref, *, f):
    nh, t = pl.program_id(0), pl.program_id(1)

    @pl.when(nrow_ref[t] > 0)
    def _():
        x = x_ref[...]
        g = jnp.dot(x, wg_ref[...], preferred_element_type=F32)
        u = jnp.dot(x, wu_ref[...], preferred_element_type=F32)
        hid = g * jax.nn.sigmoid(g) * u
        gates = gates_ref[...]
        lane = lax.broadcasted_iota(jnp.int32, gates.shape, 1)
        n_local = hid.shape[1] // f
        for e in range(n_local):
            expert = tg_ref[t] * EXPERTS_PER_GROUP + nh * n_local + e
            gate = jnp.sum(jnp.where(lane == expert, gates, 0.0), axis=-1, keepdims=True)
            o_ref[:, e * f:(e + 1) * f] = (hid[:, e * f:(e + 1) * f] * gate).astype(o_ref.dtype)


def _group_up(tile_group, tile_rows, xs, w_gate_g, w_up_g, gates_s, f):
    mp, d = xs.shape
    n_wide = w_gate_g.shape[2]
    tn = n_wide // 2
    tm = MOE_TM
    wspec = pl.BlockSpec((None, d, tn), lambda nh, t, tg, nr: (tg[t], 0, nh))
    return pl.pallas_call(
        functools.partial(_group_up_kernel, f=f),
        grid_spec=pltpu.PrefetchScalarGridSpec(
            num_scalar_prefetch=2, grid=(n_wide // tn, mp // tm),
            in_specs=[pl.BlockSpec((tm, d), lambda nh, t, tg, nr: (t, 0)), wspec, wspec,
                      pl.BlockSpec((tm, LANES), lambda nh, t, tg, nr: (t, 0))],
            out_specs=pl.BlockSpec((tm, tn), lambda nh, t, tg, nr: (t, nh))),
        out_shape=jax.ShapeDtypeStruct((mp, n_wide), BF16),
        compiler_params=_cparams("arbitrary", "arbitrary"),
        name="moe_group_up",
    )(tile_group, tile_rows, xs, w_gate_g, w_up_g, gates_s)


def _group_down_kernel(tg_ref, tok_ref, nrow_ref, h_ref, w_ref, y_ref, buf_ref, sem):
    t = pl.program_id(0)
    n_rows = nrow_ref[t]
    tm = buf_ref.shape[0]

    @pl.when(n_rows > 0)
    def _():
        buf_ref[...] = jnp.dot(h_ref[...], w_ref[...], preferred_element_type=F32)

        def copy(r, tok):
            return pltpu.make_async_copy(buf_ref.at[pl.ds(r, 1)], y_ref.at[pl.ds(tok, 1)], sem)

        def issue(r, carry):
            copy(r, tok_ref[t * tm + r]).start()
            return carry

        def drain(r, carry):
            copy(0, 0).wait()
            return carry

        lax.fori_loop(0, n_rows, issue, 0)
        lax.fori_loop(0, n_rows, drain, 0)


def _group_down(tile_group, slot_token, tile_rows, hidden_s, w_down_g, m):
    mp, n_wide = hidden_s.shape
    d = w_down_g.shape[2]
    tm = MOE_TM
    return pl.pallas_call(
        _group_down_kernel,
        grid_spec=pltpu.PrefetchScalarGridSpec(
            num_scalar_prefetch=3, grid=(mp // tm,),
            in_specs=[pl.BlockSpec((tm, n_wide), lambda t, tg, tok, nr: (t, 0)),
                      pl.BlockSpec((None, n_wide, d), lambda t, tg, tok, nr: (tg[t], 0, 0))],
            out_specs=pl.BlockSpec(memory_space=pl.ANY),
            scratch_shapes=[pltpu.VMEM((tm, d), F32), pltpu.SemaphoreType.DMA(())]),
        out_shape=jax.ShapeDtypeStruct((m, d), F32),
        compiler_params=_cparams("arbitrary"),
        name="moe_group_down",
    )(tile_group, slot_token, tile_rows, hidden_s, w_down_g)


def _routed_experts(h2, gates, w_gate, w_up, w_down):
    m, d = h2.shape
    n_exp, _, f = w_gate.shape
    tm = MOE_TM
    n_tiles = m // tm + N_GROUPS
    group = gates[:, N_EXPERTS].astype(jnp.int32)
    slot_token, tile_group, tile_rows = _routing_plan(group, tm, n_tiles)
    xs, gates_s = _gather_rows(slot_token, tile_rows, h2, gates)

    def by_group(w):
        return (w.reshape(N_GROUPS, EXPERTS_PER_GROUP, d, f).transpose(0, 2, 1, 3)
                .reshape(N_GROUPS, d, EXPERTS_PER_GROUP * f).astype(BF16))

    hidden_s = _group_up(tile_group, tile_rows, xs, by_group(w_gate), by_group(w_up), gates_s, f)
    return _group_down(tile_group, slot_token, tile_rows, hidden_s,
                       w_down.reshape(N_GROUPS, EXPERTS_PER_GROUP * f, d).astype(BF16), m)


def kernel(x, c, ctx, c_ctx, w_ada, b_ada, ln1_g, ln1_b, ln2_g, ln2_b, ev_w_in, ev_w_out, diff_lambda, diff_subln_g, gqa_q_norm_g, gqa_k_norm_g, od_w_in, mla_q_norm_g, mla_kv_norm_g, mla_w_uq, mla_w_ukv, od_w_out, moe_w_group, moe_b_group, moe_w_router, moe_b_router, moe_w_gate, moe_w_up, moe_w_down):
    batch, seq, d = x.shape
    n_ctx = ctx.shape[1]
    depth = w_ada.shape[0]
    assert batch == 1 and seq % 1024 == 0 and n_ctx == ROW_TILE and (seq + n_ctx) % MM_TM == 0
    m = seq + n_ctx
    alpha = (2.0 * depth) ** 0.25
    lat_rows, ctx_rows = (0, seq), (seq, n_ctx)
    all_rows = (0, m)

    xs = jnp.concatenate([x.reshape(seq, d), ctx.reshape(n_ctx, d)], axis=0)
    mods3 = _ada_mods(c, c_ctx, w_ada, b_ada).reshape(depth * 2 * N_MOD, 1, d)
    tab_a = _rope_tables(seq, n_ctx, HEAD_DIM)
    tab_m = _rope_tables(seq, n_ctx, MLA_ROPE)

    h = _modulate(xs, mods3, 0, 1, 0, seq)
    for layer in range(depth):
        i = layer // 2
        if layer % 2 == 0:
            lambda_init = 0.8 - 0.6 * math.exp(-0.3 * layer)
            scale = LOG2E / math.sqrt(HEAD_DIM)
            w_in = ev_w_in[i]
            n_diff_qk = d // 2
            n_diff_v = d // 2
            n_gqa_q = d // 2
            n_gqa_kv = n_gqa_q // 4
            o0 = 0
            segs = []
            for width in (n_diff_qk, n_diff_qk, n_diff_v, n_gqa_q, n_gqa_kv, n_gqa_kv):
                segs.append(w_in[:, o0:o0 + width].astype(BF16))
                o0 += width
            qa = _head_proj(h, segs[0], tn=512, tables=tab_a, scale=scale, name="proj_diff_q")
            ka = _head_proj(h, segs[1], tn=512, tables=tab_a, name="proj_diff_k")
            va = _matmul(h, segs[2], BF16, tn=512, name="proj_diff_v")
            qg = _head_proj(h, segs[3], tn=512, norm_g=gqa_q_norm_g[i], tables=tab_a, scale=scale,
                            name="proj_gqa_q")
            kg = _head_proj(h, segs[4], tn=512, norm_g=gqa_k_norm_g[i], tables=tab_a, name="proj_gqa_k")
            vg = _matmul(h, segs[5], BF16, tn=512, name="proj_gqa_v")
            parts = []
            for q_rows, kv_rows, tq_d, tq_g, tk in ((lat_rows, all_rows, ATTN_ROWS, ATTN_ROWS // 4, KV_CHUNK),
                                                    (ctx_rows, ctx_rows, n_ctx, n_ctx, n_ctx)):
                od = _diff_attention(qa, ka, va, diff_lambda[i], diff_subln_g[i], lambda_init,
                                     q_rows=q_rows, kv_rows=kv_rows, tq=tq_d, tk=tk)
                og = _gqa_attention(qg, kg, vg, q_rows=q_rows, kv_rows=kv_rows, tq=tq_g, tk=tk)
                parts.append(jnp.concatenate([od, og], axis=1))
            attn = jnp.concatenate(parts, axis=0)
            w_out = ev_w_out[i].astype(BF16)
        else:
            scale = LOG2E / math.sqrt(MLA_NOPE + MLA_ROPE)
            w_in = od_w_in[i]
            q_rank = mla_q_norm_g.shape[1]
            kv_rank = mla_kv_norm_g.shape[1]
            n_heads = mla_w_uq.shape[2] // (MLA_NOPE + MLA_ROPE)
            vec = lambda n: pl.BlockSpec((1, n), lambda j, i_: (0, 0))
            tab_specs = [pl.BlockSpec((MM_TM, LANES), lambda j, i_: (i_, 0))] * 3
            cq = _matmul(h, w_in[:, :q_rank].astype(BF16), BF16, tn=q_rank, epilogue=_rmsnorm_epilogue,
                         extras=[mla_q_norm_g[i].reshape(1, q_rank)], extra_specs=[vec(q_rank)],
                         name="proj_mla_cq")
            ckv = _matmul(h, w_in[:, q_rank:q_rank + kv_rank].astype(BF16), BF16, tn=kv_rank,
                          epilogue=_rmsnorm_epilogue, extras=[mla_kv_norm_g[i].reshape(1, kv_rank)],
                          extra_specs=[vec(kv_rank)], name="proj_mla_ckv")
            w_kr = jnp.pad(w_in[:, q_rank + kv_rank:], ((0, 0), (0, LANES - MLA_ROPE))).astype(BF16)
            kr = _matmul(h, w_kr, BF16, tn=LANES, epilogue=_rope_m_epilogue, extras=tab_m,
                         extra_specs=tab_specs, name="proj_mla_krope")
            w_uq = jnp.pad(mla_w_uq[i].reshape(q_rank, n_heads, MLA_NOPE + MLA_ROPE),
                           ((0, 0), (0, 0), (0, MLA_HEAD_PAD - MLA_NOPE - MLA_ROPE)))
            w_uq = w_uq.reshape(q_rank, n_heads * MLA_HEAD_PAD).astype(BF16)
            q = _matmul(cq, w_uq, BF16, tn=1024, epilogue=functools.partial(_mla_q_epilogue, scale=scale),
                        extras=tab_m, extra_specs=tab_specs, name="proj_mla_q")
            kv = _matmul(ckv, mla_w_ukv[i].astype(BF16), BF16, tn=1024, name="proj_mla_kv")
            attn = jnp.concatenate([
                _mla_attention(q, kv, kr, q_rows=lat_rows, kv_rows=all_rows, tq=ATTN_ROWS, tk=KV_CHUNK),
                _mla_attention(q, kv, kr, q_rows=ctx_rows, kv_rows=ctx_rows, tq=n_ctx, tk=n_ctx)], axis=0)
            w_out = od_w_out[i].astype(BF16)

        o = _matmul(attn, w_out, F32, tn=1024, name="proj_out")
        xs, h2 = _resid_ln(xs, o, mods3, layer, 2, ln1_g[layer], ln1_b[layer], layer, 4, 3, seq, alpha,
                           h_dtype=F32)

        w_rt = jnp.concatenate([moe_w_router[layer], moe_w_group[layer],
                                jnp.zeros((d, LANES - N_EXPERTS - N_GROUPS), F32)], axis=1).astype(BF16)
        b_rt = jnp.concatenate([moe_b_router[layer], moe_b_group[layer],
                                jnp.zeros((LANES - N_EXPERTS - N_GROUPS,), F32)]).reshape(1, LANES)
        gates = _router(h2, w_rt, b_rt)
        y = _routed_experts(h2, gates, moe_w_gate[layer], moe_w_up[layer], moe_w_down[layer])
        nxt = min(layer + 1, depth - 1)
        xs, h = _resid_ln(xs, y, mods3, layer, 5, ln2_g[layer], ln2_b[layer], nxt, 1, 0, seq, alpha)
    return xs[:seq].reshape(batch, seq, d)
```

```python
import functools
import math

import jax
import jax.numpy as jnp
from jax import lax
from jax.experimental import pallas as pl
from jax.experimental.pallas import tpu as pltpu

F32 = jnp.float32
BF16 = jnp.bfloat16

GRID_W = 64
HEAD_DIM = 128
ROPE_THETA = 10000.0
LN_EPS = 1e-5
RMS_EPS = 1e-6
N_GROUPS = 4
EXPERTS_PER_GROUP = 4
N_EXPERTS = N_GROUPS * EXPERTS_PER_GROUP
N_MOD = 6
MLA_NOPE = 128
MLA_ROPE = 64
MLA_V = 128
MLA_HEAD_PAD = 256
LOG2E = math.log2(math.e)

LANES = 128
V7X_VMEM_LIMIT = 56 * 2**20

ROW_TILE = 256
MM_TM = 768
KV_CHUNK = 768
ATTN_ROWS = 2048
MOE_TM = 256
ROW_DMA_UNROLL = 8


def _cparams(*sem):
    return pltpu.CompilerParams(dimension_semantics=sem, vmem_limit_bytes=V7X_VMEM_LIMIT)


def _ada_kernel(ct_ref, w_ref, b_ref, o_ref):
    ct = ct_ref[...]
    s = ct * jax.nn.sigmoid(ct)
    w = w_ref[...]
    b = b_ref[...]
    for r in range(2):
        o_ref[r:r + 1, :] = jnp.sum(w * s[:, r:r + 1], axis=0, keepdims=True) + b


def _ada_mods(c, c_ctx, w_ada, b_ada):
    depth, d, n = w_ada.shape
    tn = 512
    ct = jnp.stack([c.reshape(d), c_ctx.reshape(d)], axis=1)
    return pl.pallas_call(
        _ada_kernel,
        grid=(depth, n // tn),
        in_specs=[
            pl.BlockSpec((d, 2), lambda l, j: (0, 0)),
            pl.BlockSpec((None, d, tn), lambda l, j: (l, 0, j)),
            pl.BlockSpec((None, 1, tn), lambda l, j: (l, 0, j)),
        ],
        out_specs=pl.BlockSpec((None, 2, tn), lambda l, j: (l, 0, j)),
        out_shape=jax.ShapeDtypeStruct((depth, 2, n), F32),
        compiler_params=_cparams("arbitrary", "arbitrary"),
        name="ada_mods",
    )(ct, w_ada, b_ada.reshape(depth, 1, n))


def _mod_spec(layer, which, d, seq):
    def index(i):
        sel = jnp.where(i * ROW_TILE >= seq, 1, 0)
        return ((layer * 2 + sel) * N_MOD + which, 0, 0)
    return pl.BlockSpec((None, 1, d), index)


def _modulate_kernel(x_ref, sc_ref, sh_ref, h_ref):
    h_ref[...] = (x_ref[...] * (1.0 + sc_ref[...]) + sh_ref[...]).astype(h_ref.dtype)


def _modulate(x, mods3, layer, sc_idx, sh_idx, seq):
    m, d = x.shape
    return pl.pallas_call(
        _modulate_kernel,
        grid=(m // ROW_TILE,),
        in_specs=[pl.BlockSpec((ROW_TILE, d), lambda i: (i, 0)),
                  _mod_spec(layer, sc_idx, d, seq), _mod_spec(layer, sh_idx, d, seq)],
        out_specs=pl.BlockSpec((ROW_TILE, d), lambda i: (i, 0)),
        out_shape=jax.ShapeDtypeStruct((m, d), BF16),
        compiler_params=_cparams("arbitrary"),
        name="modulate",
    )(x, mods3, mods3)


def _resid_ln_kernel(x_ref, o_ref, g_ref, lg_ref, lb_ref, sc_ref, sh_ref, *rest, alpha, route):
    z = alpha * x_ref[...] + g_ref[...] * o_ref[...]
    mu = jnp.mean(z, axis=-1, keepdims=True)
    zc = z - mu
    var = jnp.mean(zc * zc, axis=-1, keepdims=True)
    y = zc * lax.rsqrt(var + LN_EPS) * lg_ref[...] + lb_ref[...]
    h = y * (1.0 + sc_ref[...]) + sh_ref[...]
    if route:
        w_ref, b_ref, y_ref, h_ref, gates_ref = rest
        logits = jnp.dot(h.astype(BF16), w_ref[...], preferred_element_type=F32) + b_ref[...]
        gates_ref[...] = _route(logits)
    else:
        y_ref, h_ref = rest
    y_ref[...] = y
    h_ref[...] = h.astype(h_ref.dtype)


def _resid_ln(x, o, mods3, layer, gate_idx, ln_g, ln_b, nxt_layer, sc_idx, sh_idx, seq, alpha, h_dtype=BF16,
              router=None):
    m, d = x.shape
    row = pl.BlockSpec((ROW_TILE, d), lambda i: (i, 0))
    vec = pl.BlockSpec((1, d), lambda i: (0, 0))
    in_specs = [row, row, _mod_spec(layer, gate_idx, d, seq), vec, vec,
                _mod_spec(nxt_layer, sc_idx, d, seq), _mod_spec(nxt_layer, sh_idx, d, seq)]
    args = [x, o, mods3, ln_g.reshape(1, d), ln_b.reshape(1, d), mods3, mods3]
    out_specs = [row, row]
    out_shape = [jax.ShapeDtypeStruct((m, d), F32), jax.ShapeDtypeStruct((m, d), h_dtype)]
    if router is not None:
        in_specs += [pl.BlockSpec((d, LANES), lambda i: (0, 0)), pl.BlockSpec((1, LANES), lambda i: (0, 0))]
        args += list(router)
        out_specs.append(pl.BlockSpec((ROW_TILE, LANES), lambda i: (i, 0)))
        out_shape.append(jax.ShapeDtypeStruct((m, LANES), F32))
    return pl.pallas_call(
        functools.partial(_resid_ln_kernel, alpha=alpha, route=router is not None),
        grid=(m // ROW_TILE,),
        in_specs=in_specs,
        out_specs=out_specs,
        out_shape=out_shape,
        compiler_params=_cparams("arbitrary"),
        name="resid_ln",
    )(*args)


def _mm_kernel(*refs, epilogue, n_extra, cast_b):
    a_ref, b_ref = refs[0], refs[1]
    extras = refs[2:2 + n_extra]
    o_ref = refs[2 + n_extra]
    if cast_b:
        wb_ref = refs[3 + n_extra]

        @pl.when(pl.program_id(1) == 0)
        def _():
            wb_ref[...] = b_ref[...].astype(BF16)

        b = wb_ref[...]
    else:
        b = b_ref[...]
    acc = jnp.dot(a_ref[...], b, preferred_element_type=F32)
    epilogue(acc, o_ref, *extras)


def _store(acc, o_ref):
    o_ref[...] = acc.astype(o_ref.dtype)


def _matmul(a, b, out_dtype, *, tn, n=None, col0=0, lead=None, epilogue=_store, extras=(), extra_specs=(),
            name="matmul"):
    m, k = a.shape
    n = b.shape[-1] if n is None else n
    tm = MM_TM
    tn = min(tn, n)
    assert m % tm == 0 and n % tn == 0 and col0 % tn == 0
    cast_b = b.dtype != BF16
    jb0 = col0 // tn
    if lead is None:
        b_spec = pl.BlockSpec((k, tn), lambda j, i: (0, jb0 + j))
    else:
        b_spec = pl.BlockSpec((None, k, tn), lambda j, i: (lead, 0, jb0 + j))
    return pl.pallas_call(
        functools.partial(_mm_kernel, epilogue=epilogue, n_extra=len(extras), cast_b=cast_b),
        grid=(n // tn, m // tm),
        in_specs=[pl.BlockSpec((tm, k), lambda j, i: (i, 0)), b_spec, *extra_specs],
        out_specs=pl.BlockSpec((tm, tn), lambda j, i: (i, j)),
        out_shape=jax.ShapeDtypeStruct((m, n), out_dtype),
        scratch_shapes=[pltpu.VMEM((k, tn), BF16)] if cast_b else [],
        compiler_params=_cparams("arbitrary", "arbitrary"),
        name=name,
    )(a, b, *extras)


def _rope_tables(seq, n_ctx, rot_dim):
    rows = seq // GRID_W
    quarter = rot_dim // 4
    inv_freq = ROPE_THETA ** (-jnp.arange(quarter, dtype=F32) / quarter)
    ang_r = jnp.arange(rows, dtype=F32)[:, None] * inv_freq
    ang_c = jnp.arange(GRID_W, dtype=F32)[:, None] * inv_freq

    def per_token(fn):
        by_row = jnp.broadcast_to(fn(ang_r)[:, None, :], (rows, GRID_W, quarter))
        by_col = jnp.broadcast_to(fn(ang_c)[None, :, :], (rows, GRID_W, quarter))
        return jnp.concatenate([by_row, by_col], axis=-1).reshape(seq, 2 * quarter)

    cos, sin = per_token(jnp.cos), per_token(jnp.sin)
    half = rot_dim // 2
    pad = LANES - rot_dim
    zeros_h = jnp.zeros((seq, half), F32)
    zeros_p = jnp.zeros((seq, pad), F32)
    cos_t = jnp.concatenate([cos, cos, zeros_p], axis=-1)
    sin_lo = jnp.concatenate([-sin, zeros_h, zeros_p], axis=-1)
    sin_hi = jnp.concatenate([zeros_h, sin, zeros_p], axis=-1)
    ident = jnp.concatenate([jnp.ones((n_ctx, rot_dim), F32), jnp.zeros((n_ctx, pad), F32)], axis=-1)
    zc = jnp.zeros((n_ctx, LANES), F32)
    return (jnp.concatenate([cos_t, ident], axis=0), jnp.concatenate([sin_lo, zc], axis=0),
            jnp.concatenate([sin_hi, zc], axis=0))


def _rope_block(x, cos, sin_lo, sin_hi, half):
    return (x * cos + pltpu.roll(x, LANES - half, axis=1) * sin_lo
            + pltpu.roll(x, half, axis=1) * sin_hi)


def _head_epilogue(acc, o_ref, *extras, norm, rope, scale):
    idx = 0
    if norm:
        g = extras[idx][...]
        idx += 1
    if rope:
        cos, sin_lo, sin_hi = (r[...] for r in extras[idx:idx + 3])
    for hb in range(acc.shape[1] // LANES):
        x = acc[:, hb * LANES:(hb + 1) * LANES]
        if norm:
            x = x * lax.rsqrt(jnp.mean(x * x, axis=-1, keepdims=True) + RMS_EPS) * g
        if rope:
            x = _rope_block(x, cos, sin_lo, sin_hi, HEAD_DIM // 2)
        if scale != 1.0:
            x = x * scale
        o_ref[:, hb * LANES:(hb + 1) * LANES] = x.astype(o_ref.dtype)


def _head_proj(h, w, *, tn, n, col0, lead, norm_g=None, tables=None, scale=1.0, name):
    extras, specs = [], []
    if norm_g is not None:
        extras.append(norm_g.reshape(1, LANES))
        specs.append(pl.BlockSpec((1, LANES), lambda j, i: (0, 0)))
    if tables is not None:
        extras.extend(tables)
        specs.extend([pl.BlockSpec((MM_TM, LANES), lambda j, i: (i, 0))] * 3)
    ep = functools.partial(_head_epilogue, norm=norm_g is not None, rope=tables is not None, scale=scale)
    return _matmul(h, w, BF16, tn=tn, n=n, col0=col0, lead=lead, epilogue=ep, extras=extras,
                   extra_specs=specs, name=name)


def _rmsnorm_epilogue(acc, o_ref, g_ref):
    x = acc * lax.rsqrt(jnp.mean(acc * acc, axis=-1, keepdims=True) + RMS_EPS) * g_ref[...]
    o_ref[...] = x.astype(o_ref.dtype)


def _rope_m_epilogue(acc, o_ref, cos_ref, lo_ref, hi_ref):
    o_ref[...] = _rope_block(acc, cos_ref[...], lo_ref[...], hi_ref[...], MLA_ROPE // 2).astype(o_ref.dtype)


def _mla_q_epilogue(acc, o_ref, cos_ref, lo_ref, hi_ref, *, scale):
    cos, lo, hi = cos_ref[...], lo_ref[...], hi_ref[...]
    for hd in range(acc.shape[1] // MLA_HEAD_PAD):
        base = hd * MLA_HEAD_PAD
        o_ref[:, base:base + LANES] = (acc[:, base:base + LANES] * scale).astype(o_ref.dtype)
        r = _rope_block(acc[:, base + LANES:base + 2 * LANES], cos, lo, hi, MLA_ROPE // 2)
        o_ref[:, base + LANES:base + 2 * LANES] = (r * scale).astype(o_ref.dtype)


def _softmax_step(s, v, m_ref, l_ref, acc_ref):
    m_old = m_ref[...]
    m_new = jnp.maximum(m_old, jnp.max(s, axis=-1, keepdims=True))
    p = jnp.concatenate([jnp.exp2(s[:, j * LANES:(j + 1) * LANES] - m_new)
                         for j in range(s.shape[1] // LANES)], axis=1)
    alpha = jnp.exp2(m_old - m_new)
    if l_ref is not None:
        l_ref[...] = alpha * l_ref[...] + jnp.sum(p, axis=-1, keepdims=True)
    pv = jnp.dot(p.astype(BF16), v, preferred_element_type=F32)
    acc_ref[...] = jnp.concatenate([alpha] * (acc_ref.shape[1] // LANES), axis=1) * acc_ref[...] + pv
    m_ref[...] = m_new


def _flash(q, k_ref, v_ref, m_ref, l_ref, acc_ref, s_ref, tk):
    n_chunks = k_ref.shape[0] // tk
    m_ref[...] = jnp.full(m_ref.shape, -jnp.inf, F32)
    acc_ref[...] = jnp.zeros(acc_ref.shape, F32)
    if l_ref is not None:
        l_ref[...] = jnp.zeros(l_ref.shape, F32)
    nt_dims = (((1,), (1,)), ((), ()))
    s_ref[...] = lax.dot_general(q, k_ref[pl.ds(0, tk), :], nt_dims, preferred_element_type=F32)

    def body(c, carry):
        start = pl.multiple_of(c * tk, tk)
        nxt = pl.multiple_of((c + 1) * tk, tk)
        s = s_ref[...]
        s_next = lax.dot_general(q, k_ref[pl.ds(nxt, tk), :], nt_dims, preferred_element_type=F32)
        _softmax_step(s, v_ref[pl.ds(start, tk), :], m_ref, l_ref, acc_ref)
        s_ref[...] = s_next
        return carry

    lax.fori_loop(0, n_chunks - 1, body, 0)
    _softmax_step(s_ref[...], v_ref[pl.ds((n_chunks - 1) * tk, tk), :], m_ref, l_ref, acc_ref)


def _ones_behind(v_ref, vaug_ref):
    vaug_ref[:, :LANES] = v_ref[...]
    vaug_ref[:, LANES:] = jnp.ones((v_ref.shape[0], LANES), BF16)


def _diff_attn_kernel(lam_ref, g_ref, q1_ref, q2_ref, k1_ref, k2_ref, v_ref, o_ref,
                      m_ref, l_ref, acc_ref, s_ref, a1_ref, *, tk, lambda_init):
    lv = lam_ref[...]
    lam = (jnp.exp(jnp.sum(lv[0:1] * lv[1:2], axis=-1, keepdims=True))
           - jnp.exp(jnp.sum(lv[2:3] * lv[3:4], axis=-1, keepdims=True)) + lambda_init)
    n_rep = acc_ref.shape[1] // LANES
    _flash(q1_ref[...], k1_ref, v_ref, m_ref, l_ref, acc_ref, s_ref, tk)
    a1_ref[...] = acc_ref[...] / jnp.concatenate([l_ref[...]] * n_rep, axis=1)
    _flash(q2_ref[...], k2_ref, v_ref, m_ref, l_ref, acc_ref, s_ref, tk)
    o = a1_ref[...] - lam * (acc_ref[...] / jnp.concatenate([l_ref[...]] * n_rep, axis=1))
    o = o * lax.rsqrt(jnp.mean(o * o, axis=-1, keepdims=True) + RMS_EPS) * g_ref[...]
    o_ref[...] = (o * (1.0 - lambda_init)).astype(o_ref.dtype)


def _gqa_attn_kernel(q_ref, k_ref, v_ref, o_ref, vaug_ref, qs_ref, m_ref, acc_ref, s_ref, *, tk, rep):
    tq = q_ref.shape[0]

    @pl.when(pl.program_id(1) == 0)
    def _():
        _ones_behind(v_ref, vaug_ref)

    for r in range(rep):
        qs_ref[r * tq:(r + 1) * tq, :] = q_ref[:, r * LANES:(r + 1) * LANES]
    _flash(qs_ref[...], k_ref, vaug_ref, m_ref, None, acc_ref, s_ref, tk)
    for r in range(rep):
        rows = slice(r * tq, (r + 1) * tq)
        o_ref[:, r * LANES:(r + 1) * LANES] = (acc_ref[rows, :LANES] / acc_ref[rows, LANES:]).astype(o_ref.dtype)


def _mla_attn_kernel(q_ref, kn_ref, kr_ref, v_ref, o_ref, kcat_ref, vaug_ref, m_ref, acc_ref, s_ref, *, tk):
    @pl.when(pl.program_id(1) == 0)
    def _():
        kcat_ref[:, :LANES] = kn_ref[...]
        kcat_ref[:, LANES:] = kr_ref[...]
        _ones_behind(v_ref, vaug_ref)

    _flash(q_ref[...], kcat_ref, vaug_ref, m_ref, None, acc_ref, s_ref, tk)
    o_ref[...] = (acc_ref[:, :LANES] / acc_ref[:, LANES:]).astype(o_ref.dtype)


def _diff_attention(qa, ka, va, lam_vecs, subln_g, lambda_init, *, q_rows, kv_rows, tq, tk):
    n_heads = va.shape[1] // (2 * HEAD_DIM)
    q0, nq = q_rows
    k0, nk = kv_rows
    qb, kb = q0 // tq, k0 // nk
    dv = 2 * HEAD_DIM
    return pl.pallas_call(
        functools.partial(_diff_attn_kernel, tk=tk, lambda_init=lambda_init),
        grid=(n_heads, nq // tq),
        in_specs=[
            pl.BlockSpec((4, HEAD_DIM), lambda h, i: (0, 0)),
            pl.BlockSpec((1, dv), lambda h, i: (0, 0)),
            pl.BlockSpec((tq, HEAD_DIM), lambda h, i: (qb + i, 2 * h)),
            pl.BlockSpec((tq, HEAD_DIM), lambda h, i: (qb + i, 2 * h + 1)),
            pl.BlockSpec((nk, HEAD_DIM), lambda h, i: (kb, 2 * h)),
            pl.BlockSpec((nk, HEAD_DIM), lambda h, i: (kb, 2 * h + 1)),
            pl.BlockSpec((nk, dv), lambda h, i: (kb, h)),
        ],
        out_specs=pl.BlockSpec((tq, dv), lambda h, i: (i, h)),
        out_shape=jax.ShapeDtypeStruct((nq, n_heads * dv), BF16),
        scratch_shapes=[pltpu.VMEM((tq, LANES), F32), pltpu.VMEM((tq, LANES), F32), pltpu.VMEM((tq, dv), F32),
                        pltpu.VMEM((tq, tk), F32), pltpu.VMEM((tq, dv), F32)],
        compiler_params=_cparams("arbitrary", "arbitrary"),
        name="diff_attention",
    )(lam_vecs, subln_g.reshape(1, dv), qa, qa, ka, ka, va)


def _gqa_attention(qg, kg, vg, *, q_rows, kv_rows, tq, tk):
    n_kv = kg.shape[1] // HEAD_DIM
    rep = qg.shape[1] // kg.shape[1]
    q0, nq = q_rows
    k0, nk = kv_rows
    qb, kb = q0 // tq, k0 // nk
    rows = rep * tq
    return pl.pallas_call(
        functools.partial(_gqa_attn_kernel, tk=tk, rep=rep),
        grid=(n_kv, nq // tq),
        in_specs=[
            pl.BlockSpec((tq, rep * HEAD_DIM), lambda g, i: (qb + i, g)),
            pl.BlockSpec((nk, HEAD_DIM), lambda g, i: (kb, g)),
            pl.BlockSpec((nk, HEAD_DIM), lambda g, i: (kb, g)),
        ],
        out_specs=pl.BlockSpec((tq, rep * HEAD_DIM), lambda g, i: (i, g)),
        out_shape=jax.ShapeDtypeStruct((nq, qg.shape[1]), BF16),
        scratch_shapes=[pltpu.VMEM((nk, 2 * LANES), BF16), pltpu.VMEM((rows, HEAD_DIM), BF16),
                        pltpu.VMEM((rows, LANES), F32), pltpu.VMEM((rows, 2 * LANES), F32),
                        pltpu.VMEM((rows, tk), F32)],
        compiler_params=_cparams("arbitrary", "arbitrary"),
        name="gqa_attention",
    )(qg, kg, vg)


def _mla_attention(q, kv, kr, *, q_rows, kv_rows, tq, tk):
    n_heads = q.shape[1] // MLA_HEAD_PAD
    q0, nq = q_rows
    k0, nk = kv_rows
    qb, kb = q0 // tq, k0 // nk
    return pl.pallas_call(
        functools.partial(_mla_attn_kernel, tk=tk),
        grid=(n_heads, nq // tq),
        in_specs=[
            pl.BlockSpec((tq, MLA_HEAD_PAD), lambda h, i: (qb + i, h)),
            pl.BlockSpec((nk, MLA_NOPE), lambda h, i: (kb, 2 * h)),
            pl.BlockSpec((nk, LANES), lambda h, i: (kb, 0)),
            pl.BlockSpec((nk, MLA_V), lambda h, i: (kb, 2 * h + 1)),
        ],
        out_specs=pl.BlockSpec((tq, MLA_V), lambda h, i: (i, h)),
        out_shape=jax.ShapeDtypeStruct((nq, n_heads * MLA_V), BF16),
        scratch_shapes=[pltpu.VMEM((nk, MLA_HEAD_PAD), BF16), pltpu.VMEM((nk, 2 * LANES), BF16),
                        pltpu.VMEM((tq, LANES), F32), pltpu.VMEM((tq, 2 * LANES), F32),
                        pltpu.VMEM((tq, tk), F32)],
        compiler_params=_cparams("arbitrary", "arbitrary"),
        name="mla_attention",
    )(q, kv, kr, kv)


def _route(logits):
    lane = lax.broadcasted_iota(jnp.int32, logits.shape, 1).astype(F32)
    neg = jnp.float32(-jnp.inf)
    big = jnp.float32(LANES)

    gmask = (lane >= N_EXPERTS) & (lane < N_EXPERTS + N_GROUPS)
    gl = jnp.where(gmask, logits, neg)
    ge = jnp.exp(gl - jnp.max(gl, axis=-1, keepdims=True))
    gprob = ge / jnp.sum(ge, axis=-1, keepdims=True)
    group_w = jnp.max(gprob, axis=-1, keepdims=True)
    gidx = jnp.min(jnp.where(gmask & (gprob == group_w), lane, big), axis=-1, keepdims=True) - N_EXPERTS

    emask = (lane >= gidx * EXPERTS_PER_GROUP) & (lane < (gidx + 1) * EXPERTS_PER_GROUP)
    el = jnp.where(emask, logits, neg)
    ee = jnp.exp(el - jnp.max(el, axis=-1, keepdims=True))
    ep = ee / jnp.sum(ee, axis=-1, keepdims=True)
    p1 = jnp.max(ep, axis=-1, keepdims=True)
    i1 = jnp.min(jnp.where(emask & (ep == p1), lane, big), axis=-1, keepdims=True)
    rest = emask & (lane != i1)
    p2 = jnp.max(jnp.where(rest, ep, neg), axis=-1, keepdims=True)
    i2 = jnp.min(jnp.where(rest & (ep == p2), lane, big), axis=-1, keepdims=True)
    denom = p1 + p2
    return (jnp.where(lane == i1, group_w * (p1 / denom), 0.0)
            + jnp.where(lane == i2, group_w * (p2 / denom), 0.0)
            + jnp.where(lane == N_EXPERTS, gidx, 0.0))


def _routing_plan(group, tm, n_tiles):
    m = group.shape[0]
    onehot = (group[:, None] == jnp.arange(N_GROUPS, dtype=jnp.int32)[None, :]).astype(jnp.int32)
    counts = jnp.sum(onehot, axis=0)
    padded = (counts + tm - 1) // tm * tm
    ends = jnp.cumsum(padded)
    starts = ends - padded
    rank = jnp.take_along_axis(jnp.cumsum(onehot, axis=0), group[:, None], axis=1)[:, 0] - 1
    slot = starts[group] + rank
    slot_token = jnp.zeros((n_tiles * tm,), jnp.int32).at[slot].set(jnp.arange(m, dtype=jnp.int32))
    tile_start = jnp.arange(n_tiles, dtype=jnp.int32) * tm
    tile_group = jnp.minimum(jnp.searchsorted(ends, tile_start, side="right"), N_GROUPS - 1)
    tile_rows = jnp.clip((starts + counts)[tile_group] - tile_start, 0, tm)
    return slot_token, tile_group.astype(jnp.int32), tile_rows.astype(jnp.int32)


def _for_rows(n_rows, tm, body):
    def step(r, carry):
        body(r)
        return carry

    @pl.when(n_rows == tm)
    def _():
        lax.fori_loop(0, tm, step, 0, unroll=ROW_DMA_UNROLL)

    @pl.when(n_rows < tm)
    def _():
        lax.fori_loop(0, n_rows, step, 0)


def _gather_rows_kernel(tok_ref, nrow_ref, x_ref, g_ref, xo_ref, go_ref, buf_ref, sems):
    t = pl.program_id(0)
    n_rows = nrow_ref[t]
    tm = buf_ref.shape[0]

    @pl.when(t == 0)
    def _():
        buf_ref[...] = jnp.zeros(buf_ref.shape, buf_ref.dtype)

    go_ref[...] = jnp.zeros(go_ref.shape, go_ref.dtype)

    def copies(r, tok):
        return (pltpu.make_async_copy(x_ref.at[pl.ds(tok, 1)], buf_ref.at[pl.ds(r, 1)], sems.at[0]),
                pltpu.make_async_copy(g_ref.at[pl.ds(tok, 1)], go_ref.at[pl.ds(r, 1)], sems.at[1]))

    def issue(r):
        for cp in copies(r, tok_ref[t * tm + r]):
            cp.start()

    def drain(r):
        for cp in copies(0, 0):
            cp.wait()

    _for_rows(n_rows, tm, issue)
    _for_rows(n_rows, tm, drain)
    xo_ref[...] = buf_ref[...].astype(xo_ref.dtype)


def _gather_rows(slot_token, tile_rows, x, gates):
    n_slots = slot_token.shape[0]
    d = x.shape[1]
    tm = MOE_TM
    any_spec = pl.BlockSpec(memory_space=pl.ANY)
    return pl.pallas_call(
        _gather_rows_kernel,
        grid_spec=pltpu.PrefetchScalarGridSpec(
            num_scalar_prefetch=2, grid=(n_slots // tm,),
            in_specs=[any_spec, any_spec],
            out_specs=[pl.BlockSpec((tm, d), lambda t, tok, nr: (t, 0)),
                       pl.BlockSpec((tm, LANES), lambda t, tok, nr: (t, 0))],
            scratch_shapes=[pltpu.VMEM((tm, d), x.dtype), pltpu.SemaphoreType.DMA((2,))]),
        out_shape=[jax.ShapeDtypeStruct((n_slots, d), BF16), jax.ShapeDtypeStruct((n_slots, LANES), F32)],
        compiler_params=_cparams("arbitrary"),
        name="moe_gather_rows",
    )(slot_token, tile_rows, x, gates)


def _new_weights(tg_ref, t):
    return jnp.logical_or(t == 0, tg_ref[t] != tg_ref[jnp.maximum(t - 1, 0)])


def _group_up_kernel(tg_ref, nrow_ref, x_ref, wga_ref, wgb_ref, wua_ref, wub_ref, gates_ref, o_ref,
                     wg_ref, wu_ref):
    nh, t = pl.program_id(0), pl.program_id(1)
    f = wga_ref.shape[1]

    @pl.when(_new_weights(tg_ref, t))
    def _():
        wg_ref[:, :f] = wga_ref[...].astype(BF16)
        wg_ref[:, f:] = wgb_ref[...].astype(BF16)
        wu_ref[:, :f] = wua_ref[...].astype(BF16)
        wu_ref[:, f:] = wub_ref[...].astype(BF16)

    @pl.when(nrow_ref[t] > 0)
    def _():
        x = x_ref[...]
        g = jnp.dot(x, wg_ref[...], preferred_element_type=F32)
        u = jnp.dot(x, wu_ref[...], preferred_element_type=F32)
        hid = g * jax.nn.sigmoid(g) * u
        gates = gates_ref[...]
        lane = lax.broadcasted_iota(jnp.int32, gates.shape, 1)
        for e in range(2):
            expert = tg_ref[t] * EXPERTS_PER_GROUP + nh * 2 + e
            gate = jnp.sum(jnp.where(lane == expert, gates, 0.0), axis=-1, keepdims=True)
            o_ref[:, e * f:(e + 1) * f] = (hid[:, e * f:(e + 1) * f] * gate).astype(o_ref.dtype)


def _group_up(tile_group, tile_rows, xs, w_gate, w_up, gates_s, layer):
    mp, d = xs.shape
    f = w_gate.shape[3]
    tm = MOE_TM

    def wspec(e):
        return pl.BlockSpec((None, None, d, f),
                            lambda nh, t, tg, nr: (layer, tg[t] * EXPERTS_PER_GROUP + nh * 2 + e, 0, 0),
                            pipeline_mode=pl.Buffered(1))

    return pl.pallas_call(
        _group_up_kernel,
        grid_spec=pltpu.PrefetchScalarGridSpec(
            num_scalar_prefetch=2, grid=(EXPERTS_PER_GROUP // 2, mp // tm),
            in_specs=[pl.BlockSpec((tm, d), lambda nh, t, tg, nr: (t, 0)),
                      wspec(0), wspec(1), wspec(0), wspec(1),
                      pl.BlockSpec((tm, LANES), lambda nh, t, tg, nr: (t, 0))],
            out_specs=pl.BlockSpec((tm, 2 * f), lambda nh, t, tg, nr: (t, nh)),
            scratch_shapes=[pltpu.VMEM((d, 2 * f), BF16), pltpu.VMEM((d, 2 * f), BF16)]),
        out_shape=jax.ShapeDtypeStruct((mp, EXPERTS_PER_GROUP * f), BF16),
        compiler_params=_cparams("arbitrary", "arbitrary"),
        name="moe_group_up",
    )(tile_group, tile_rows, xs, w_gate, w_gate, w_up, w_up, gates_s)


def _group_down_kernel(tg_ref, tok_ref, nrow_ref, h_ref, w_ref, y_ref, wb_ref, buf_ref, sem):
    t = pl.program_id(0)
    n_rows = nrow_ref[t]
    tm = buf_ref.shape[0]

    @pl.when(_new_weights(tg_ref, t))
    def _():
        wb_ref[...] = w_ref[...].astype(BF16)

    @pl.when(n_rows > 0)
    def _():
        buf_ref[...] = jnp.dot(h_ref[...], wb_ref[...], preferred_element_type=F32)

        def copy(r, tok):
            return pltpu.make_async_copy(buf_ref.at[pl.ds(r, 1)], y_ref.at[pl.ds(tok, 1)], sem)

        _for_rows(n_rows, tm, lambda r: copy(r, tok_ref[t * tm + r]).start())
        _for_rows(n_rows, tm, lambda r: copy(0, 0).wait())


def _group_down(tile_group, slot_token, tile_rows, hidden_s, w_down_g, m, layer):
    mp, n_wide = hidden_s.shape
    d = w_down_g.shape[3]
    tm = MOE_TM
    return pl.pallas_call(
        _group_down_kernel,
        grid_spec=pltpu.PrefetchScalarGridSpec(
            num_scalar_prefetch=3, grid=(mp // tm,),
            in_specs=[pl.BlockSpec((tm, n_wide), lambda t, tg, tok, nr: (t, 0)),
                      pl.BlockSpec((None, None, n_wide, d), lambda t, tg, tok, nr: (layer, tg[t], 0, 0),
                                   pipeline_mode=pl.Buffered(1))],
            out_specs=pl.BlockSpec(memory_space=pl.ANY),
            scratch_shapes=[pltpu.VMEM((n_wide, d), BF16), pltpu.VMEM((tm, d), F32),
                            pltpu.SemaphoreType.DMA(())]),
        out_shape=jax.ShapeDtypeStruct((m, d), F32),
        compiler_params=_cparams("arbitrary"),
        name="moe_group_down",
    )(tile_group, slot_token, tile_rows, hidden_s, w_down_g)


def _routed_experts(h2, gates, w_gate, w_up, w_down, layer):
    m, d = h2.shape
    depth, n_exp, _, f = w_gate.shape
    tm = MOE_TM
    n_tiles = m // tm + N_GROUPS
    group = gates[:, N_EXPERTS].astype(jnp.int32)
    slot_token, tile_group, tile_rows = _routing_plan(group, tm, n_tiles)
    xs, gates_s = _gather_rows(slot_token, tile_rows, h2, gates)
    hidden_s = _group_up(tile_group, tile_rows, xs, w_gate, w_up, gates_s, layer)
    return _group_down(tile_group, slot_token, tile_rows, hidden_s,
                       w_down.reshape(depth, N_GROUPS, EXPERTS_PER_GROUP * f, d), m, layer)


def kernel(x, c, ctx, c_ctx, w_ada, b_ada, ln1_g, ln1_b, ln2_g, ln2_b, ev_w_in, ev_w_out, diff_lambda, diff_subln_g, gqa_q_norm_g, gqa_k_norm_g, od_w_in, mla_q_norm_g, mla_kv_norm_g, mla_w_uq, mla_w_ukv, od_w_out, moe_w_group, moe_b_group, moe_w_router, moe_b_router, moe_w_gate, moe_w_up, moe_w_down):
    batch, seq, d = x.shape
    n_ctx = ctx.shape[1]
    depth = w_ada.shape[0]
    assert batch == 1 and seq % 1024 == 0 and n_ctx == ROW_TILE and (seq + n_ctx) % MM_TM == 0
    m = seq + n_ctx
    alpha = (2.0 * depth) ** 0.25
    lat_rows, ctx_rows = (0, seq), (seq, n_ctx)
    all_rows = (0, m)

    xs = jnp.concatenate([x.reshape(seq, d), ctx.reshape(n_ctx, d)], axis=0)
    mods3 = _ada_mods(c, c_ctx, w_ada, b_ada).reshape(depth * 2 * N_MOD, 1, d)
    tab_a = _rope_tables(seq, n_ctx, HEAD_DIM)
    tab_m = _rope_tables(seq, n_ctx, MLA_ROPE)

    h = _modulate(xs, mods3, 0, 1, 0, seq)
    for layer in range(depth):
        i = layer // 2
        if layer % 2 == 0:
            lambda_init = 0.8 - 0.6 * math.exp(-0.3 * layer)
            scale = LOG2E / math.sqrt(HEAD_DIM)
            n_diff_qk = d // 2
            n_diff_v = d // 2
            n_gqa_q = d // 2
            n_gqa_kv = n_gqa_q // 4
            widths = (n_diff_qk, n_diff_qk, n_diff_v, n_gqa_q, n_gqa_kv, n_gqa_kv)
            c0 = [sum(widths[:s]) for s in range(len(widths))]
            seg = lambda s: dict(tn=512, n=widths[s], col0=c0[s], lead=i)
            qa = _head_proj(h, ev_w_in, **seg(0), tables=tab_a, scale=scale, name="proj_diff_q")
            ka = _head_proj(h, ev_w_in, **seg(1), tables=tab_a, name="proj_diff_k")
            va = _matmul(h, ev_w_in, BF16, **seg(2), name="proj_diff_v")
            qg = _head_proj(h, ev_w_in, **seg(3), norm_g=gqa_q_norm_g[i], tables=tab_a, scale=scale,
                            name="proj_gqa_q")
            kg = _head_proj(h, ev_w_in, **seg(4), norm_g=gqa_k_norm_g[i], tables=tab_a, name="proj_gqa_k")
            vg = _matmul(h, ev_w_in, BF16, **seg(5), name="proj_gqa_v")
            parts = []
            for q_rows, kv_rows, tq_d, tq_g, tk in ((lat_rows, all_rows, ATTN_ROWS, ATTN_ROWS // 4, KV_CHUNK),
                                                    (ctx_rows, ctx_rows, n_ctx, n_ctx, n_ctx)):
                od = _diff_attention(qa, ka, va, diff_lambda[i], diff_subln_g[i], lambda_init,
                                     q_rows=q_rows, kv_rows=kv_rows, tq=tq_d, tk=tk)
                og = _gqa_attention(qg, kg, vg, q_rows=q_rows, kv_rows=kv_rows, tq=tq_g, tk=tk)
                parts.append(jnp.concatenate([od, og], axis=1))
            attn = jnp.concatenate(parts, axis=0)
            w_out = ev_w_out
        else:
            scale = LOG2E / math.sqrt(MLA_NOPE + MLA_ROPE)
            w_in = od_w_in[i]
            q_rank = mla_q_norm_g.shape[1]
            kv_rank = mla_kv_norm_g.shape[1]
            n_heads = mla_w_uq.shape[2] // (MLA_NOPE + MLA_ROPE)
            vec = lambda n: pl.BlockSpec((1, n), lambda j, i_: (0, 0))
            tab_specs = [pl.BlockSpec((MM_TM, LANES), lambda j, i_: (i_, 0))] * 3
            cq = _matmul(h, w_in[:, :q_rank].astype(BF16), BF16, tn=q_rank, epilogue=_rmsnorm_epilogue,
                         extras=[mla_q_norm_g[i].reshape(1, q_rank)], extra_specs=[vec(q_rank)],
                         name="proj_mla_cq")
            ckv = _matmul(h, od_w_in, BF16, tn=kv_rank, n=kv_rank, col0=q_rank, lead=i,
                          epilogue=_rmsnorm_epilogue, extras=[mla_kv_norm_g[i].reshape(1, kv_rank)],
                          extra_specs=[vec(kv_rank)], name="proj_mla_ckv")
            w_kr = jnp.pad(w_in[:, q_rank + kv_rank:], ((0, 0), (0, LANES - MLA_ROPE))).astype(BF16)
            kr = _matmul(h, w_kr, BF16, tn=LANES, epilogue=_rope_m_epilogue, extras=tab_m,
                         extra_specs=tab_specs, name="proj_mla_krope")
            w_uq = jnp.pad(mla_w_uq[i].reshape(q_rank, n_heads, MLA_NOPE + MLA_ROPE),
                           ((0, 0), (0, 0), (0, MLA_HEAD_PAD - MLA_NOPE - MLA_ROPE)))
            w_uq = w_uq.reshape(q_rank, n_heads * MLA_HEAD_PAD).astype(BF16)
            q = _matmul(cq, w_uq, BF16, tn=1024, epilogue=functools.partial(_mla_q_epilogue, scale=scale),
                        extras=tab_m, extra_specs=tab_specs, name="proj_mla_q")
            kv = _matmul(ckv, mla_w_ukv, BF16, tn=1024, lead=i, name="proj_mla_kv")
            attn = jnp.concatenate([
                _mla_attention(q, kv, kr, q_rows=lat_rows, kv_rows=all_rows, tq=ATTN_ROWS, tk=KV_CHUNK),
                _mla_attention(q, kv, kr, q_rows=ctx_rows, kv_rows=ctx_rows, tq=n_ctx, tk=n_ctx)], axis=0)
            w_out = od_w_out

        o = _matmul(attn, w_out, F32, tn=512, lead=i, name="proj_out")
        w_rt = jnp.concatenate([moe_w_router[layer], moe_w_group[layer],
                                jnp.zeros((d, LANES - N_EXPERTS - N_GROUPS), F32)], axis=1).astype(BF16)
        b_rt = jnp.concatenate([moe_b_router[layer], moe_b_group[layer],
                                jnp.zeros((LANES - N_EXPERTS - N_GROUPS,), F32)]).reshape(1, LANES)
        xs, h2, gates = _resid_ln(xs, o, mods3, layer, 2, ln1_g[layer], ln1_b[layer], layer, 4, 3, seq, alpha,
                                  h_dtype=F32, router=(w_rt, b_rt))
        y = _routed_experts(h2, gates, moe_w_gate, moe_w_up, moe_w_down, layer)
        nxt = min(layer + 1, depth - 1)
        xs, h = _resid_ln(xs, y, mods3, layer, 5, ln2_g[layer], ln2_b[layer], nxt, 1, 0, seq, alpha)
    return xs[:seq].reshape(batch, seq, d)
```

```python
import functools
import math

import jax
import jax.numpy as jnp
from jax import lax
from jax.experimental import pallas as pl
from jax.experimental.pallas import tpu as pltpu

F32 = jnp.float32
BF16 = jnp.bfloat16

GRID_W = 64
HEAD_DIM = 128
ROPE_THETA = 10000.0
LN_EPS = 1e-5
RMS_EPS = 1e-6
N_GROUPS = 4
EXPERTS_PER_GROUP = 4
N_EXPERTS = N_GROUPS * EXPERTS_PER_GROUP
N_MOD = 6
MLA_NOPE = 128
MLA_ROPE = 64
MLA_V = 128
MLA_HEAD_PAD = 256
LOG2E = math.log2(math.e)

LANES = 128
V7X_VMEM_LIMIT = 56 * 2**20

ROW_TILE = 256
MM_TM = 768
KV_CHUNK = 768
ATTN_ROWS = 2048
FLASH_ROW_BLOCK = 512
MOE_TM = 256
ROW_DMA_UNROLL = 8


def _cparams(*sem):
    return pltpu.CompilerParams(dimension_semantics=sem, vmem_limit_bytes=V7X_VMEM_LIMIT)


def _ada_kernel(ct_ref, w_ref, b_ref, o_ref):
    ct = ct_ref[...]
    s = ct * jax.nn.sigmoid(ct)
    w = w_ref[...]
    b = b_ref[...]
    for r in range(2):
        o_ref[r:r + 1, :] = jnp.sum(w * s[:, r:r + 1], axis=0, keepdims=True) + b


def _ada_mods(c, c_ctx, w_ada, b_ada):
    depth, d, n = w_ada.shape
    tn = 512
    ct = jnp.stack([c.reshape(d), c_ctx.reshape(d)], axis=1)
    return pl.pallas_call(
        _ada_kernel,
        grid=(depth, n // tn),
        in_specs=[
            pl.BlockSpec((d, 2), lambda l, j: (0, 0)),
            pl.BlockSpec((None, d, tn), lambda l, j: (l, 0, j)),
            pl.BlockSpec((None, 1, tn), lambda l, j: (l, 0, j)),
        ],
        out_specs=pl.BlockSpec((None, 2, tn), lambda l, j: (l, 0, j)),
        out_shape=jax.ShapeDtypeStruct((depth, 2, n), F32),
        compiler_params=_cparams("arbitrary", "arbitrary"),
        name="ada_mods",
    )(ct, w_ada, b_ada.reshape(depth, 1, n))


def _mod_spec(layer, which, d, seq):
    def index(i):
        sel = jnp.where(i * ROW_TILE >= seq, 1, 0)
        return ((layer * 2 + sel) * N_MOD + which, 0, 0)
    return pl.BlockSpec((None, 1, d), index)


def _modulate_kernel(x_ref, sc_ref, sh_ref, h_ref):
    h_ref[...] = (x_ref[...] * (1.0 + sc_ref[...]) + sh_ref[...]).astype(h_ref.dtype)


def _modulate(x, mods3, layer, sc_idx, sh_idx, seq):
    m, d = x.shape
    return pl.pallas_call(
        _modulate_kernel,
        grid=(m // ROW_TILE,),
        in_specs=[pl.BlockSpec((ROW_TILE, d), lambda i: (i, 0)),
                  _mod_spec(layer, sc_idx, d, seq), _mod_spec(layer, sh_idx, d, seq)],
        out_specs=pl.BlockSpec((ROW_TILE, d), lambda i: (i, 0)),
        out_shape=jax.ShapeDtypeStruct((m, d), BF16),
        compiler_params=_cparams("arbitrary"),
        name="modulate",
    )(x, mods3, mods3)


def _resid_ln_kernel(x_ref, o_ref, g_ref, lg_ref, lb_ref, sc_ref, sh_ref, *rest, alpha, route):
    z = alpha * x_ref[...] + g_ref[...] * o_ref[...]
    mu = jnp.mean(z, axis=-1, keepdims=True)
    zc = z - mu
    var = jnp.mean(zc * zc, axis=-1, keepdims=True)
    y = zc * lax.rsqrt(var + LN_EPS) * lg_ref[...] + lb_ref[...]
    h = y * (1.0 + sc_ref[...]) + sh_ref[...]
    if route:
        w_ref, b_ref, y_ref, h_ref, gates_ref = rest
        logits = jnp.dot(h.astype(BF16), w_ref[...], preferred_element_type=F32) + b_ref[...]
        gates_ref[...] = _route(logits)
    else:
        y_ref, h_ref = rest
    y_ref[...] = y
    h_ref[...] = h.astype(h_ref.dtype)


def _resid_ln(x, o, mods3, layer, gate_idx, ln_g, ln_b, nxt_layer, sc_idx, sh_idx, seq, alpha, h_dtype=BF16,
              router=None, n_rows=None):
    m, d = x.shape
    m = m if n_rows is None else n_rows
    row = pl.BlockSpec((ROW_TILE, d), lambda i: (i, 0))
    vec = pl.BlockSpec((1, d), lambda i: (0, 0))
    in_specs = [row, row, _mod_spec(layer, gate_idx, d, seq), vec, vec,
                _mod_spec(nxt_layer, sc_idx, d, seq), _mod_spec(nxt_layer, sh_idx, d, seq)]
    args = [x, o, mods3, ln_g.reshape(1, d), ln_b.reshape(1, d), mods3, mods3]
    out_specs = [row, row]
    out_shape = [jax.ShapeDtypeStruct((m, d), F32), jax.ShapeDtypeStruct((m, d), h_dtype)]
    if router is not None:
        in_specs += [pl.BlockSpec((d, LANES), lambda i: (0, 0)), pl.BlockSpec((1, LANES), lambda i: (0, 0))]
        args += list(router)
        out_specs.append(pl.BlockSpec((ROW_TILE, LANES), lambda i: (i, 0)))
        out_shape.append(jax.ShapeDtypeStruct((m, LANES), F32))
    return pl.pallas_call(
        functools.partial(_resid_ln_kernel, alpha=alpha, route=router is not None),
        grid=(m // ROW_TILE,),
        in_specs=in_specs,
        out_specs=out_specs,
        out_shape=out_shape,
        compiler_params=_cparams("arbitrary"),
        name="resid_ln",
    )(*args)


def _mm_kernel(*refs, epilogue, n_extra, cast_b):
    a_ref, b_ref = refs[0], refs[1]
    extras = refs[2:2 + n_extra]
    o_ref = refs[2 + n_extra]
    if cast_b:
        wb_ref = refs[3 + n_extra]

        @pl.when(pl.program_id(1) == 0)
        def _():
            wb_ref[...] = b_ref[...].astype(BF16)

        b = wb_ref[...]
    else:
        b = b_ref[...]
    acc = jnp.dot(a_ref[...], b, preferred_element_type=F32)
    epilogue(acc, o_ref, *extras)


def _store(acc, o_ref):
    o_ref[...] = acc.astype(o_ref.dtype)


def _matmul(a, b, out_dtype, *, tn, n=None, col0=0, lead=None, epilogue=_store, extras=(), extra_specs=(),
            name="matmul"):
    m, k = a.shape
    n = b.shape[-1] if n is None else n
    tm = MM_TM
    tn = min(tn, n)
    assert m % tm == 0 and n % tn == 0 and col0 % tn == 0
    cast_b = b.dtype != BF16
    jb0 = col0 // tn
    if lead is None:
        b_spec = pl.BlockSpec((k, tn), lambda j, i: (0, jb0 + j))
    else:
        b_spec = pl.BlockSpec((None, k, tn), lambda j, i: (lead, 0, jb0 + j))
    return pl.pallas_call(
        functools.partial(_mm_kernel, epilogue=epilogue, n_extra=len(extras), cast_b=cast_b),
        grid=(n // tn, m // tm),
        in_specs=[pl.BlockSpec((tm, k), lambda j, i: (i, 0)), b_spec, *extra_specs],
        out_specs=pl.BlockSpec((tm, tn), lambda j, i: (i, j)),
        out_shape=jax.ShapeDtypeStruct((m, n), out_dtype),
        scratch_shapes=[pltpu.VMEM((k, tn), BF16)] if cast_b else [],
        compiler_params=_cparams("arbitrary", "arbitrary"),
        name=name,
    )(a, b, *extras)


def _rope_tables(seq, n_ctx, rot_dim):
    rows = seq // GRID_W
    quarter = rot_dim // 4
    inv_freq = ROPE_THETA ** (-jnp.arange(quarter, dtype=F32) / quarter)
    ang_r = jnp.arange(rows, dtype=F32)[:, None] * inv_freq
    ang_c = jnp.arange(GRID_W, dtype=F32)[:, None] * inv_freq

    def per_token(fn):
        by_row = jnp.broadcast_to(fn(ang_r)[:, None, :], (rows, GRID_W, quarter))
        by_col = jnp.broadcast_to(fn(ang_c)[None, :, :], (rows, GRID_W, quarter))
        return jnp.concatenate([by_row, by_col], axis=-1).reshape(seq, 2 * quarter)

    cos, sin = per_token(jnp.cos), per_token(jnp.sin)
    half = rot_dim // 2
    pad = LANES - rot_dim
    zeros_h = jnp.zeros((seq, half), F32)
    zeros_p = jnp.zeros((seq, pad), F32)
    cos_t = jnp.concatenate([cos, cos, zeros_p], axis=-1)
    sin_lo = jnp.concatenate([-sin, zeros_h, zeros_p], axis=-1)
    sin_hi = jnp.concatenate([zeros_h, sin, zeros_p], axis=-1)
    ident = jnp.concatenate([jnp.ones((n_ctx, rot_dim), F32), jnp.zeros((n_ctx, pad), F32)], axis=-1)
    zc = jnp.zeros((n_ctx, LANES), F32)
    return (jnp.concatenate([cos_t, ident], axis=0), jnp.concatenate([sin_lo, zc], axis=0),
            jnp.concatenate([sin_hi, zc], axis=0))


def _rope_block(x, cos, sin_lo, sin_hi, half):
    return (x * cos + pltpu.roll(x, LANES - half, axis=1) * sin_lo
            + pltpu.roll(x, half, axis=1) * sin_hi)


def _head_epilogue(acc, o_ref, *extras, norm, rope, scale):
    idx = 0
    if norm:
        g = extras[idx][...]
        idx += 1
    if rope:
        cos, sin_lo, sin_hi = (r[...] for r in extras[idx:idx + 3])
    for hb in range(acc.shape[1] // LANES):
        x = acc[:, hb * LANES:(hb + 1) * LANES]
        if norm:
            x = x * lax.rsqrt(jnp.mean(x * x, axis=-1, keepdims=True) + RMS_EPS) * g
        if rope:
            x = _rope_block(x, cos, sin_lo, sin_hi, HEAD_DIM // 2)
        if scale != 1.0:
            x = x * scale
        o_ref[:, hb * LANES:(hb + 1) * LANES] = x.astype(o_ref.dtype)


def _head_proj(h, w, *, tn, n, col0, lead, norm_g=None, tables=None, scale=1.0, name):
    extras, specs = [], []
    if norm_g is not None:
        extras.append(norm_g.reshape(1, LANES))
        specs.append(pl.BlockSpec((1, LANES), lambda j, i: (0, 0)))
    if tables is not None:
        extras.extend(tables)
        specs.extend([pl.BlockSpec((MM_TM, LANES), lambda j, i: (i, 0))] * 3)
    ep = functools.partial(_head_epilogue, norm=norm_g is not None, rope=tables is not None, scale=scale)
    return _matmul(h, w, BF16, tn=tn, n=n, col0=col0, lead=lead, epilogue=ep, extras=extras,
                   extra_specs=specs, name=name)


def _rmsnorm_epilogue(acc, o_ref, g_ref):
    x = acc * lax.rsqrt(jnp.mean(acc * acc, axis=-1, keepdims=True) + RMS_EPS) * g_ref[...]
    o_ref[...] = x.astype(o_ref.dtype)


def _rope_m_epilogue(acc, o_ref, cos_ref, lo_ref, hi_ref):
    o_ref[...] = _rope_block(acc, cos_ref[...], lo_ref[...], hi_ref[...], MLA_ROPE // 2).astype(o_ref.dtype)


def _mla_q_epilogue(acc, o_ref, cos_ref, lo_ref, hi_ref, *, scale):
    cos, lo, hi = cos_ref[...], lo_ref[...], hi_ref[...]
    for hd in range(acc.shape[1] // MLA_HEAD_PAD):
        base = hd * MLA_HEAD_PAD
        o_ref[:, base:base + LANES] = (acc[:, base:base + LANES] * scale).astype(o_ref.dtype)
        r = _rope_block(acc[:, base + LANES:base + 2 * LANES], cos, lo, hi, MLA_ROPE // 2)
        o_ref[:, base + LANES:base + 2 * LANES] = (r * scale).astype(o_ref.dtype)


def _softmax_step(s, v, m_ref, l_ref, acc_ref, rows):
    m_old = m_ref[rows, :]
    m_new = jnp.maximum(m_old, jnp.max(s, axis=-1, keepdims=True))
    p = jnp.concatenate([jnp.exp2(s[:, j * LANES:(j + 1) * LANES] - m_new)
                         for j in range(s.shape[1] // LANES)], axis=1)
    alpha = jnp.exp2(m_old - m_new)
    if l_ref is not None:
        l_ref[rows, :] = alpha * l_ref[rows, :] + jnp.sum(p, axis=-1, keepdims=True)
    pv = jnp.dot(p.astype(BF16), v, preferred_element_type=F32)
    acc_ref[rows, :] = jnp.concatenate([alpha] * (acc_ref.shape[1] // LANES), axis=1) * acc_ref[rows, :] + pv
    m_ref[rows, :] = m_new


def _flash(q, k_ref, v_ref, m_ref, l_ref, acc_ref, s_ref, tk):
    n_chunks = k_ref.shape[0] // tk
    n_rows = q.shape[0]
    rb = min(FLASH_ROW_BLOCK, n_rows)
    blocks = [pl.ds(b * rb, rb) for b in range(n_rows // rb)]
    qs = [q[b * rb:(b + 1) * rb] for b in range(n_rows // rb)]
    m_ref[...] = jnp.full(m_ref.shape, -jnp.inf, F32)
    acc_ref[...] = jnp.zeros(acc_ref.shape, F32)
    if l_ref is not None:
        l_ref[...] = jnp.zeros(l_ref.shape, F32)
    nt_dims = (((1,), (1,)), ((), ()))
    s_ref[...] = lax.dot_general(q, k_ref[pl.ds(0, tk), :], nt_dims, preferred_element_type=F32)

    def body(c, carry):
        start = pl.multiple_of(c * tk, tk)
        nxt = pl.multiple_of((c + 1) * tk, tk)
        k_next = k_ref[pl.ds(nxt, tk), :]
        v = v_ref[pl.ds(start, tk), :]
        for rows, q_b in zip(blocks, qs):
            s = s_ref[rows, :]
            s_next = lax.dot_general(q_b, k_next, nt_dims, preferred_element_type=F32)
            _softmax_step(s, v, m_ref, l_ref, acc_ref, rows)
            s_ref[rows, :] = s_next
        return carry

    lax.fori_loop(0, n_chunks - 1, body, 0)
    v = v_ref[pl.ds((n_chunks - 1) * tk, tk), :]
    for rows in blocks:
        _softmax_step(s_ref[rows, :], v, m_ref, l_ref, acc_ref, rows)


def _ones_behind(v_ref, vaug_ref):
    vaug_ref[:, :LANES] = v_ref[...]
    vaug_ref[:, LANES:] = jnp.ones((v_ref.shape[0], LANES), BF16)


def _diff_attn_kernel(lam_ref, g_ref, q1_ref, q2_ref, k1_ref, k2_ref, v_ref, _prev_ref, o_ref,
                      m_ref, l_ref, acc_ref, s_ref, a1_ref, *, tk, lambda_init):
    lv = lam_ref[...]
    lam = (jnp.exp(jnp.sum(lv[0:1] * lv[1:2], axis=-1, keepdims=True))
           - jnp.exp(jnp.sum(lv[2:3] * lv[3:4], axis=-1, keepdims=True)) + lambda_init)
    n_rep = acc_ref.shape[1] // LANES
    _flash(q1_ref[...], k1_ref, v_ref, m_ref, l_ref, acc_ref, s_ref, tk)
    a1_ref[...] = acc_ref[...] / jnp.concatenate([l_ref[...]] * n_rep, axis=1)
    _flash(q2_ref[...], k2_ref, v_ref, m_ref, l_ref, acc_ref, s_ref, tk)
    o = a1_ref[...] - lam * (acc_ref[...] / jnp.concatenate([l_ref[...]] * n_rep, axis=1))
    o = o * lax.rsqrt(jnp.mean(o * o, axis=-1, keepdims=True) + RMS_EPS) * g_ref[...]
    o_ref[...] = (o * (1.0 - lambda_init)).astype(o_ref.dtype)


def _gqa_attn_kernel(q_ref, k_ref, v_ref, _prev_ref, o_ref, vaug_ref, qs_ref, m_ref, acc_ref, s_ref, *,
                     tk, rep):
    tq = q_ref.shape[0]

    @pl.when(pl.program_id(1) == 0)
    def _():
        _ones_behind(v_ref, vaug_ref)

    for r in range(rep):
        qs_ref[r * tq:(r + 1) * tq, :] = q_ref[:, r * LANES:(r + 1) * LANES]
    _flash(qs_ref[...], k_ref, vaug_ref, m_ref, None, acc_ref, s_ref, tk)
    for r in range(rep):
        rows = slice(r * tq, (r + 1) * tq)
        o_ref[:, r * LANES:(r + 1) * LANES] = (acc_ref[rows, :LANES] / acc_ref[rows, LANES:]).astype(o_ref.dtype)


def _mla_attn_kernel(q_ref, kn_ref, kr_ref, v_ref, _prev_ref, o_ref, kcat_ref, vaug_ref, m_ref, acc_ref,
                     s_ref, *, tk):
    @pl.when(pl.program_id(1) == 0)
    def _():
        kcat_ref[:, :LANES] = kn_ref[...]
        kcat_ref[:, LANES:] = kr_ref[...]
        _ones_behind(v_ref, vaug_ref)

    _flash(q_ref[...], kcat_ref, vaug_ref, m_ref, None, acc_ref, s_ref, tk)
    o_ref[...] = (acc_ref[:, :LANES] / acc_ref[:, LANES:]).astype(o_ref.dtype)


def _diff_attention(qa, ka, va, lam_vecs, subln_g, lambda_init, out, col0, *, q_rows, kv_rows, tq, tk):
    n_heads = va.shape[1] // (2 * HEAD_DIM)
    q0, nq = q_rows
    k0, nk = kv_rows
    qb, kb = q0 // tq, k0 // nk
    dv = 2 * HEAD_DIM
    cb = col0 // dv
    return pl.pallas_call(
        functools.partial(_diff_attn_kernel, tk=tk, lambda_init=lambda_init),
        grid=(n_heads, nq // tq),
        in_specs=[
            pl.BlockSpec((4, HEAD_DIM), lambda h, i: (0, 0)),
            pl.BlockSpec((1, dv), lambda h, i: (0, 0)),
            pl.BlockSpec((tq, HEAD_DIM), lambda h, i: (qb + i, 2 * h)),
            pl.BlockSpec((tq, HEAD_DIM), lambda h, i: (qb + i, 2 * h + 1)),
            pl.BlockSpec((nk, HEAD_DIM), lambda h, i: (kb, 2 * h)),
            pl.BlockSpec((nk, HEAD_DIM), lambda h, i: (kb, 2 * h + 1)),
            pl.BlockSpec((nk, dv), lambda h, i: (kb, h)),
            pl.BlockSpec(memory_space=pl.ANY),
        ],
        out_specs=pl.BlockSpec((tq, dv), lambda h, i: (qb + i, cb + h)),
        out_shape=jax.ShapeDtypeStruct(out.shape, out.dtype),
        input_output_aliases={7: 0},
        scratch_shapes=[pltpu.VMEM((tq, LANES), F32), pltpu.VMEM((tq, LANES), F32), pltpu.VMEM((tq, dv), F32),
                        pltpu.VMEM((tq, tk), F32), pltpu.VMEM((tq, dv), F32)],
        compiler_params=_cparams("arbitrary", "arbitrary"),
        name="diff_attention",
    )(lam_vecs, subln_g.reshape(1, dv), qa, qa, ka, ka, va, out)


def _gqa_attention(qg, kg, vg, out, col0, *, q_rows, kv_rows, tq, tk):
    n_kv = kg.shape[1] // HEAD_DIM
    rep = qg.shape[1] // kg.shape[1]
    q0, nq = q_rows
    k0, nk = kv_rows
    qb, kb = q0 // tq, k0 // nk
    rows = rep * tq
    cb = col0 // (rep * HEAD_DIM)
    return pl.pallas_call(
        functools.partial(_gqa_attn_kernel, tk=tk, rep=rep),
        grid=(n_kv, nq // tq),
        in_specs=[
            pl.BlockSpec((tq, rep * HEAD_DIM), lambda g, i: (qb + i, g)),
            pl.BlockSpec((nk, HEAD_DIM), lambda g, i: (kb, g)),
            pl.BlockSpec((nk, HEAD_DIM), lambda g, i: (kb, g)),
            pl.BlockSpec(memory_space=pl.ANY),
        ],
        out_specs=pl.BlockSpec((tq, rep * HEAD_DIM), lambda g, i: (qb + i, cb + g)),
        out_shape=jax.ShapeDtypeStruct(out.shape, out.dtype),
        input_output_aliases={3: 0},
        scratch_shapes=[pltpu.VMEM((nk, 2 * LANES), BF16), pltpu.VMEM((rows, HEAD_DIM), BF16),
                        pltpu.VMEM((rows, LANES), F32), pltpu.VMEM((rows, 2 * LANES), F32),
                        pltpu.VMEM((rows, tk), F32)],
        compiler_params=_cparams("arbitrary", "arbitrary"),
        name="gqa_attention",
    )(qg, kg, vg, out)


def _mla_attention(q, kv, kr, out, *, q_rows, kv_rows, tq, tk):
    n_heads = q.shape[1] // MLA_HEAD_PAD
    q0, nq = q_rows
    k0, nk = kv_rows
    qb, kb = q0 // tq, k0 // nk
    return pl.pallas_call(
        functools.partial(_mla_attn_kernel, tk=tk),
        grid=(n_heads, nq // tq),
        in_specs=[
            pl.BlockSpec((tq, MLA_HEAD_PAD), lambda h, i: (qb + i, h)),
            pl.BlockSpec((nk, MLA_NOPE), lambda h, i: (kb, 2 * h)),
            pl.BlockSpec((nk, LANES), lambda h, i: (kb, 0)),
            pl.BlockSpec((nk, MLA_V), lambda h, i: (kb, 2 * h + 1)),
            pl.BlockSpec(memory_space=pl.ANY),
        ],
        out_specs=pl.BlockSpec((tq, MLA_V), lambda h, i: (qb + i, h)),
        out_shape=jax.ShapeDtypeStruct(out.shape, out.dtype),
        input_output_aliases={4: 0},
        scratch_shapes=[pltpu.VMEM((nk, MLA_HEAD_PAD), BF16), pltpu.VMEM((nk, 2 * LANES), BF16),
                        pltpu.VMEM((tq, LANES), F32), pltpu.VMEM((tq, 2 * LANES), F32),
                        pltpu.VMEM((tq, tk), F32)],
        compiler_params=_cparams("arbitrary", "arbitrary"),
        name="mla_attention",
    )(q, kv, kr, kv, out)


def _route(logits):
    lane = lax.broadcasted_iota(jnp.int32, logits.shape, 1).astype(F32)
    neg = jnp.float32(-jnp.inf)
    big = jnp.float32(LANES)

    gmask = (lane >= N_EXPERTS) & (lane < N_EXPERTS + N_GROUPS)
    gl = jnp.where(gmask, logits, neg)
    ge = jnp.exp(gl - jnp.max(gl, axis=-1, keepdims=True))
    gprob = ge / jnp.sum(ge, axis=-1, keepdims=True)
    group_w = jnp.max(gprob, axis=-1, keepdims=True)
    gidx = jnp.min(jnp.where(gmask & (gprob == group_w), lane, big), axis=-1, keepdims=True) - N_EXPERTS

    emask = (lane >= gidx * EXPERTS_PER_GROUP) & (lane < (gidx + 1) * EXPERTS_PER_GROUP)
    el = jnp.where(emask, logits, neg)
    ee = jnp.exp(el - jnp.max(el, axis=-1, keepdims=True))
    ep = ee / jnp.sum(ee, axis=-1, keepdims=True)
    p1 = jnp.max(ep, axis=-1, keepdims=True)
    i1 = jnp.min(jnp.where(emask & (ep == p1), lane, big), axis=-1, keepdims=True)
    rest = emask & (lane != i1)
    p2 = jnp.max(jnp.where(rest, ep, neg), axis=-1, keepdims=True)
    i2 = jnp.min(jnp.where(rest & (ep == p2), lane, big), axis=-1, keepdims=True)
    denom = p1 + p2
    return (jnp.where(lane == i1, group_w * (p1 / denom), 0.0)
            + jnp.where(lane == i2, group_w * (p2 / denom), 0.0)
            + jnp.where(lane == N_EXPERTS, gidx, 0.0))


def _routing_plan(group, tm, n_tiles):
    m = group.shape[0]
    onehot = (group[:, None] == jnp.arange(N_GROUPS, dtype=jnp.int32)[None, :]).astype(jnp.int32)
    counts = jnp.sum(onehot, axis=0)
    padded = (counts + tm - 1) // tm * tm
    ends = jnp.cumsum(padded)
    starts = ends - padded
    rank = jnp.take_along_axis(jnp.cumsum(onehot, axis=0), group[:, None], axis=1)[:, 0] - 1
    slot = starts[group] + rank
    slot_token = jnp.zeros((n_tiles * tm,), jnp.int32).at[slot].set(jnp.arange(m, dtype=jnp.int32))
    tile_start = jnp.arange(n_tiles, dtype=jnp.int32) * tm
    tile_group = jnp.minimum(jnp.searchsorted(ends, tile_start, side="right"), N_GROUPS - 1)
    tile_rows = jnp.clip((starts + counts)[tile_group] - tile_start, 0, tm)
    return slot_token, tile_group.astype(jnp.int32), tile_rows.astype(jnp.int32)


def _for_rows(n_rows, tm, body):
    def step(r, carry):
        body(r)
        return carry

    @pl.when(n_rows == tm)
    def _():
        lax.fori_loop(0, tm, step, 0, unroll=ROW_DMA_UNROLL)

    @pl.when(n_rows < tm)
    def _():
        lax.fori_loop(0, n_rows, step, 0)


def _gather_rows_kernel(tok_ref, nrow_ref, x_ref, g_ref, xo_ref, go_ref, buf_ref, sems):
    t = pl.program_id(0)
    n_rows = nrow_ref[t]
    tm = buf_ref.shape[0]

    @pl.when(t == 0)
    def _():
        buf_ref[...] = jnp.zeros(buf_ref.shape, buf_ref.dtype)

    go_ref[...] = jnp.zeros(go_ref.shape, go_ref.dtype)

    def copies(r, tok):
        return (pltpu.make_async_copy(x_ref.at[pl.ds(tok, 1)], buf_ref.at[pl.ds(r, 1)], sems.at[0]),
                pltpu.make_async_copy(g_ref.at[pl.ds(tok, 1)], go_ref.at[pl.ds(r, 1)], sems.at[1]))

    def issue(r):
        for cp in copies(r, tok_ref[t * tm + r]):
            cp.start()

    def drain(r):
        for cp in copies(0, 0):
            cp.wait()

    _for_rows(n_rows, tm, issue)
    _for_rows(n_rows, tm, drain)
    xo_ref[...] = buf_ref[...].astype(xo_ref.dtype)


def _gather_rows(slot_token, tile_rows, x, gates):
    n_slots = slot_token.shape[0]
    d = x.shape[1]
    tm = MOE_TM
    any_spec = pl.BlockSpec(memory_space=pl.ANY)
    return pl.pallas_call(
        _gather_rows_kernel,
        grid_spec=pltpu.PrefetchScalarGridSpec(
            num_scalar_prefetch=2, grid=(n_slots // tm,),
            in_specs=[any_spec, any_spec],
            out_specs=[pl.BlockSpec((tm, d), lambda t, tok, nr: (t, 0)),
                       pl.BlockSpec((tm, LANES), lambda t, tok, nr: (t, 0))],
            scratch_shapes=[pltpu.VMEM((tm, d), x.dtype), pltpu.SemaphoreType.DMA((2,))]),
        out_shape=[jax.ShapeDtypeStruct((n_slots, d), BF16), jax.ShapeDtypeStruct((n_slots, LANES), F32)],
        compiler_params=_cparams("arbitrary"),
        name="moe_gather_rows",
    )(slot_token, tile_rows, x, gates)


def _new_weights(tg_ref, t):
    return jnp.logical_or(t == 0, tg_ref[t] != tg_ref[jnp.maximum(t - 1, 0)])


def _group_up_kernel(tg_ref, nrow_ref, x_ref, wga_ref, wgb_ref, wua_ref, wub_ref, gates_ref, o_ref,
                     wg_ref, wu_ref):
    nh, t = pl.program_id(0), pl.program_id(1)
    f = wga_ref.shape[1]

    @pl.when(_new_weights(tg_ref, t))
    def _():
        wg_ref[:, :f] = wga_ref[...].astype(BF16)
        wg_ref[:, f:] = wgb_ref[...].astype(BF16)
        wu_ref[:, :f] = wua_ref[...].astype(BF16)
        wu_ref[:, f:] = wub_ref[...].astype(BF16)

    @pl.when(nrow_ref[t] > 0)
    def _():
        x = x_ref[...]
        g = jnp.dot(x, wg_ref[...], preferred_element_type=F32)
        u = jnp.dot(x, wu_ref[...], preferred_element_type=F32)
        hid = g * jax.nn.sigmoid(g) * u
        gates = gates_ref[...]
        lane = lax.broadcasted_iota(jnp.int32, gates.shape, 1)
        for e in range(2):
            expert = tg_ref[t] * EXPERTS_PER_GROUP + nh * 2 + e
            gate = jnp.sum(jnp.where(lane == expert, gates, 0.0), axis=-1, keepdims=True)
            o_ref[:, e * f:(e + 1) * f] = (hid[:, e * f:(e + 1) * f] * gate).astype(o_ref.dtype)


def _group_up(tile_group, tile_rows, xs, w_gate, w_up, gates_s, layer):
    mp, d = xs.shape
    f = w_gate.shape[3]
    tm = MOE_TM

    def wspec(e):
        return pl.BlockSpec((None, None, d, f),
                            lambda nh, t, tg, nr: (layer, tg[t] * EXPERTS_PER_GROUP + nh * 2 + e, 0, 0),
                            pipeline_mode=pl.Buffered(1))

    return pl.pallas_call(
        _group_up_kernel,
        grid_spec=pltpu.PrefetchScalarGridSpec(
            num_scalar_prefetch=2, grid=(EXPERTS_PER_GROUP // 2, mp // tm),
            in_specs=[pl.BlockSpec((tm, d), lambda nh, t, tg, nr: (t, 0)),
                      wspec(0), wspec(1), wspec(0), wspec(1),
                      pl.BlockSpec((tm, LANES), lambda nh, t, tg, nr: (t, 0))],
            out_specs=pl.BlockSpec((tm, 2 * f), lambda nh, t, tg, nr: (t, nh)),
            scratch_shapes=[pltpu.VMEM((d, 2 * f), BF16), pltpu.VMEM((d, 2 * f), BF16)]),
        out_shape=jax.ShapeDtypeStruct((mp, EXPERTS_PER_GROUP * f), BF16),
        compiler_params=_cparams("arbitrary", "arbitrary"),
        name="moe_group_up",
    )(tile_group, tile_rows, xs, w_gate, w_gate, w_up, w_up, gates_s)


def _group_down_kernel(tg_ref, tok_ref, nrow_ref, h_ref, w_ref, y_ref, wb_ref, buf_ref, sem):
    t = pl.program_id(0)
    n_rows = nrow_ref[t]
    tm = buf_ref.shape[0]

    @pl.when(_new_weights(tg_ref, t))
    def _():
        wb_ref[...] = w_ref[...].astype(BF16)

    @pl.when(n_rows > 0)
    def _():
        buf_ref[...] = jnp.dot(h_ref[...], wb_ref[...], preferred_element_type=F32)

        def copy(r, tok):
            return pltpu.make_async_copy(buf_ref.at[pl.ds(r, 1)], y_ref.at[pl.ds(tok, 1)], sem)

        _for_rows(n_rows, tm, lambda r: copy(r, tok_ref[t * tm + r]).start())
        _for_rows(n_rows, tm, lambda r: copy(0, 0).wait())


def _group_down(tile_group, slot_token, tile_rows, hidden_s, w_down_g, m, layer):
    mp, n_wide = hidden_s.shape
    d = w_down_g.shape[3]
    tm = MOE_TM
    return pl.pallas_call(
        _group_down_kernel,
        grid_spec=pltpu.PrefetchScalarGridSpec(
            num_scalar_prefetch=3, grid=(mp // tm,),
            in_specs=[pl.BlockSpec((tm, n_wide), lambda t, tg, tok, nr: (t, 0)),
                      pl.BlockSpec((None, None, n_wide, d), lambda t, tg, tok, nr: (layer, tg[t], 0, 0),
                                   pipeline_mode=pl.Buffered(1))],
            out_specs=pl.BlockSpec(memory_space=pl.ANY),
            scratch_shapes=[pltpu.VMEM((n_wide, d), BF16), pltpu.VMEM((tm, d), F32),
                            pltpu.SemaphoreType.DMA(())]),
        out_shape=jax.ShapeDtypeStruct((m, d), F32),
        compiler_params=_cparams("arbitrary"),
        name="moe_group_down",
    )(tile_group, slot_token, tile_rows, hidden_s, w_down_g)


def _routed_experts(h2, gates, w_gate, w_up, w_down, layer):
    m, d = h2.shape
    depth, n_exp, _, f = w_gate.shape
    tm = MOE_TM
    n_tiles = m // tm + N_GROUPS
    group = gates[:, N_EXPERTS].astype(jnp.int32)
    slot_token, tile_group, tile_rows = _routing_plan(group, tm, n_tiles)
    xs, gates_s = _gather_rows(slot_token, tile_rows, h2, gates)
    hidden_s = _group_up(tile_group, tile_rows, xs, w_gate, w_up, gates_s, layer)
    return _group_down(tile_group, slot_token, tile_rows, hidden_s,
                       w_down.reshape(depth, N_GROUPS, EXPERTS_PER_GROUP * f, d), m, layer)


def kernel(x, c, ctx, c_ctx, w_ada, b_ada, ln1_g, ln1_b, ln2_g, ln2_b, ev_w_in, ev_w_out, diff_lambda, diff_subln_g, gqa_q_norm_g, gqa_k_norm_g, od_w_in, mla_q_norm_g, mla_kv_norm_g, mla_w_uq, mla_w_ukv, od_w_out, moe_w_group, moe_b_group, moe_w_router, moe_b_router, moe_w_gate, moe_w_up, moe_w_down):
    batch, seq, d = x.shape
    n_ctx = ctx.shape[1]
    depth = w_ada.shape[0]
    assert batch == 1 and seq % 1024 == 0 and n_ctx == ROW_TILE and (seq + n_ctx) % MM_TM == 0
    m = seq + n_ctx
    alpha = (2.0 * depth) ** 0.25
    lat_rows, ctx_rows = (0, seq), (seq, n_ctx)
    all_rows = (0, m)

    xs = jnp.concatenate([x.reshape(seq, d), ctx.reshape(n_ctx, d)], axis=0)
    mods3 = _ada_mods(c, c_ctx, w_ada, b_ada).reshape(depth * 2 * N_MOD, 1, d)
    tab_a = _rope_tables(seq, n_ctx, HEAD_DIM)
    tab_m = _rope_tables(seq, n_ctx, MLA_ROPE)

    h = _modulate(xs, mods3, 0, 1, 0, seq)
    for layer in range(depth):
        i = layer // 2
        if layer % 2 == 0:
            lambda_init = 0.8 - 0.6 * math.exp(-0.3 * layer)
            scale = LOG2E / math.sqrt(HEAD_DIM)
            n_diff_qk = d // 2
            n_diff_v = d // 2
            n_gqa_q = d // 2
            n_gqa_kv = n_gqa_q // 4
            widths = (n_diff_qk, n_diff_qk, n_diff_v, n_gqa_q, n_gqa_kv, n_gqa_kv)
            c0 = [sum(widths[:s]) for s in range(len(widths))]
            seg = lambda s: dict(tn=512, n=widths[s], col0=c0[s], lead=i)
            qa = _head_proj(h, ev_w_in, **seg(0), tables=tab_a, scale=scale, name="proj_diff_q")
            ka = _head_proj(h, ev_w_in, **seg(1), tables=tab_a, name="proj_diff_k")
            va = _matmul(h, ev_w_in, BF16, **seg(2), name="proj_diff_v")
            qg = _head_proj(h, ev_w_in, **seg(3), norm_g=gqa_q_norm_g[i], tables=tab_a, scale=scale,
                            name="proj_gqa_q")
            kg = _head_proj(h, ev_w_in, **seg(4), norm_g=gqa_k_norm_g[i], tables=tab_a, name="proj_gqa_k")
            vg = _matmul(h, ev_w_in, BF16, **seg(5), name="proj_gqa_v")
            attn = jnp.zeros((m, d), BF16)
            for q_rows, kv_rows, tq_d, tq_g, tk in ((lat_rows, all_rows, ATTN_ROWS, ATTN_ROWS // 4, KV_CHUNK),
                                                    (ctx_rows, ctx_rows, n_ctx, n_ctx, n_ctx)):
                attn = _diff_attention(qa, ka, va, diff_lambda[i], diff_subln_g[i], lambda_init, attn, 0,
                                       q_rows=q_rows, kv_rows=kv_rows, tq=tq_d, tk=tk)
                attn = _gqa_attention(qg, kg, vg, attn, n_diff_v, q_rows=q_rows, kv_rows=kv_rows, tq=tq_g,
                                      tk=tk)
            w_out = ev_w_out
        else:
            scale = LOG2E / math.sqrt(MLA_NOPE + MLA_ROPE)
            w_in = od_w_in[i]
            q_rank = mla_q_norm_g.shape[1]
            kv_rank = mla_kv_norm_g.shape[1]
            n_heads = mla_w_uq.shape[2] // (MLA_NOPE + MLA_ROPE)
            vec = lambda n: pl.BlockSpec((1, n), lambda j, i_: (0, 0))
            tab_specs = [pl.BlockSpec((MM_TM, LANES), lambda j, i_: (i_, 0))] * 3
            cq = _matmul(h, w_in[:, :q_rank].astype(BF16), BF16, tn=q_rank, epilogue=_rmsnorm_epilogue,
                         extras=[mla_q_norm_g[i].reshape(1, q_rank)], extra_specs=[vec(q_rank)],
                         name="proj_mla_cq")
            ckv = _matmul(h, od_w_in, BF16, tn=kv_rank, n=kv_rank, col0=q_rank, lead=i,
                          epilogue=_rmsnorm_epilogue, extras=[mla_kv_norm_g[i].reshape(1, kv_rank)],
                          extra_specs=[vec(kv_rank)], name="proj_mla_ckv")
            w_kr = jnp.pad(w_in[:, q_rank + kv_rank:], ((0, 0), (0, LANES - MLA_ROPE))).astype(BF16)
            kr = _matmul(h, w_kr, BF16, tn=LANES, epilogue=_rope_m_epilogue, extras=tab_m,
                         extra_specs=tab_specs, name="proj_mla_krope")
            w_uq = jnp.pad(mla_w_uq[i].reshape(q_rank, n_heads, MLA_NOPE + MLA_ROPE),
                           ((0, 0), (0, 0), (0, MLA_HEAD_PAD - MLA_NOPE - MLA_ROPE)))
            w_uq = w_uq.reshape(q_rank, n_heads * MLA_HEAD_PAD).astype(BF16)
            q = _matmul(cq, w_uq, BF16, tn=1024, epilogue=functools.partial(_mla_q_epilogue, scale=scale),
                        extras=tab_m, extra_specs=tab_specs, name="proj_mla_q")
            kv = _matmul(ckv, mla_w_ukv, BF16, tn=1024, lead=i, name="proj_mla_kv")
            attn = jnp.zeros((m, n_heads * MLA_V), BF16)
            attn = _mla_attention(q, kv, kr, attn, q_rows=lat_rows, kv_rows=all_rows, tq=ATTN_ROWS, tk=KV_CHUNK)
            attn = _mla_attention(q, kv, kr, attn, q_rows=ctx_rows, kv_rows=ctx_rows, tq=n_ctx, tk=n_ctx)
            w_out = od_w_out

        o = _matmul(attn, w_out, F32, tn=512, lead=i, name="proj_out")
        w_rt = jnp.concatenate([moe_w_router[layer], moe_w_group[layer],
                                jnp.zeros((d, LANES - N_EXPERTS - N_GROUPS), F32)], axis=1).astype(BF16)
        b_rt = jnp.concatenate([moe_b_router[layer], moe_b_group[layer],
                                jnp.zeros((LANES - N_EXPERTS - N_GROUPS,), F32)]).reshape(1, LANES)
        xs, h2, gates = _resid_ln(xs, o, mods3, layer, 2, ln1_g[layer], ln1_b[layer], layer, 4, 3, seq, alpha,
                                  h_dtype=F32, router=(w_rt, b_rt))
        y = _routed_experts(h2, gates, moe_w_gate, moe_w_up, moe_w_down, layer)
        nxt = min(layer + 1, depth - 1)
        last = layer == depth - 1
        xs, h = _resid_ln(xs, y, mods3, layer, 5, ln2_g[layer], ln2_b[layer], nxt, 1, 0, seq, alpha,
                          n_rows=seq if last else None)
    return xs.reshape(batch, seq, d)
```

```python
import functools
import math

import jax
import jax.numpy as jnp
from jax import lax
from jax.experimental import pallas as pl
from jax.experimental.pallas import tpu as pltpu

F32 = jnp.float32
BF16 = jnp.bfloat16

GRID_W = 64
HEAD_DIM = 128
ROPE_THETA = 10000.0
LN_EPS = 1e-5
RMS_EPS = 1e-6
N_GROUPS = 4
EXPERTS_PER_GROUP = 4
N_EXPERTS = N_GROUPS * EXPERTS_PER_GROUP
N_MOD = 6
MLA_NOPE = 128
MLA_ROPE = 64
MLA_V = 128
MLA_HEAD_PAD = 256
LOG2E = math.log2(math.e)

LANES = 128
V7X_VMEM_LIMIT = 56 * 2**20

ROW_TILE = 256
MM_TM = 768
MM_TN = 512
MM_TN_WIDE = 1024
KV_CHUNK = 768
ATTN_ROWS = 2048
FLASH_ROW_BLOCK = 512
MOE_TM = 256
ROW_DMA_UNROLL = 8


def _cparams(*sem):
    return pltpu.CompilerParams(dimension_semantics=sem, vmem_limit_bytes=V7X_VMEM_LIMIT)


def _ada_kernel(ct_ref, w_ref, b_ref, o_ref):
    ct = ct_ref[...]
    s = ct * jax.nn.sigmoid(ct)
    w = w_ref[...]
    b = b_ref[...]
    for r in range(2):
        o_ref[r:r + 1, :] = jnp.sum(w * s[:, r:r + 1], axis=0, keepdims=True) + b


def _ada_mods(c, c_ctx, w_ada, b_ada):
    depth, d, n = w_ada.shape
    tn = 512
    ct = jnp.stack([c.reshape(d), c_ctx.reshape(d)], axis=1)
    return pl.pallas_call(
        _ada_kernel,
        grid=(depth, n // tn),
        in_specs=[
            pl.BlockSpec((d, 2), lambda l, j: (0, 0)),
            pl.BlockSpec((None, d, tn), lambda l, j: (l, 0, j)),
            pl.BlockSpec((None, 1, tn), lambda l, j: (l, 0, j)),
        ],
        out_specs=pl.BlockSpec((None, 2, tn), lambda l, j: (l, 0, j)),
        out_shape=jax.ShapeDtypeStruct((depth, 2, n), F32),
        compiler_params=_cparams("arbitrary", "arbitrary"),
        name="ada_mods",
    )(ct, w_ada, b_ada.reshape(depth, 1, n))


def _mod_spec(layer, which, d, seq):
    def index(i):
        sel = jnp.where(i * ROW_TILE >= seq, 1, 0)
        return ((layer * 2 + sel) * N_MOD + which, 0, 0)
    return pl.BlockSpec((None, 1, d), index)


def _modulate_kernel(x_ref, sc_ref, sh_ref, h_ref):
    h_ref[...] = (x_ref[...] * (1.0 + sc_ref[...]) + sh_ref[...]).astype(h_ref.dtype)


def _modulate(x, mods3, layer, sc_idx, sh_idx, seq):
    m, d = x.shape
    return pl.pallas_call(
        _modulate_kernel,
        grid=(m // ROW_TILE,),
        in_specs=[pl.BlockSpec((ROW_TILE, d), lambda i: (i, 0)),
                  _mod_spec(layer, sc_idx, d, seq), _mod_spec(layer, sh_idx, d, seq)],
        out_specs=pl.BlockSpec((ROW_TILE, d), lambda i: (i, 0)),
        out_shape=jax.ShapeDtypeStruct((m, d), BF16),
        compiler_params=_cparams("arbitrary"),
        name="modulate",
    )(x, mods3, mods3)


def _resid_ln_kernel(x_ref, o_ref, g_ref, lg_ref, lb_ref, sc_ref, sh_ref, *rest, alpha, route):
    z = alpha * x_ref[...] + g_ref[...] * o_ref[...]
    mu = jnp.mean(z, axis=-1, keepdims=True)
    zc = z - mu
    var = jnp.mean(zc * zc, axis=-1, keepdims=True)
    y = zc * lax.rsqrt(var + LN_EPS) * lg_ref[...] + lb_ref[...]
    h = y * (1.0 + sc_ref[...]) + sh_ref[...]
    if route:
        w_ref, b_ref, y_ref, h_ref, gates_ref = rest
        logits = jnp.dot(h.astype(BF16), w_ref[...], preferred_element_type=F32) + b_ref[...]
        gates_ref[...] = _route(logits)
    else:
        y_ref, h_ref = rest
    y_ref[...] = y
    h_ref[...] = h.astype(h_ref.dtype)


def _resid_ln(x, o, mods3, layer, gate_idx, ln_g, ln_b, nxt_layer, sc_idx, sh_idx, seq, alpha, h_dtype=BF16,
              router=None, n_rows=None):
    m, d = x.shape
    m = m if n_rows is None else n_rows
    row = pl.BlockSpec((ROW_TILE, d), lambda i: (i, 0))
    vec = pl.BlockSpec((1, d), lambda i: (0, 0))
    in_specs = [row, row, _mod_spec(layer, gate_idx, d, seq), vec, vec,
                _mod_spec(nxt_layer, sc_idx, d, seq), _mod_spec(nxt_layer, sh_idx, d, seq)]
    args = [x, o, mods3, ln_g.reshape(1, d), ln_b.reshape(1, d), mods3, mods3]
    out_specs = [row, row]
    out_shape = [jax.ShapeDtypeStruct((m, d), F32), jax.ShapeDtypeStruct((m, d), h_dtype)]
    if router is not None:
        in_specs += [pl.BlockSpec((d, LANES), lambda i: (0, 0)), pl.BlockSpec((1, LANES), lambda i: (0, 0))]
        args += list(router)
        out_specs.append(pl.BlockSpec((ROW_TILE, LANES), lambda i: (i, 0)))
        out_shape.append(jax.ShapeDtypeStruct((m, LANES), F32))
    return pl.pallas_call(
        functools.partial(_resid_ln_kernel, alpha=alpha, route=router is not None),
        grid=(m // ROW_TILE,),
        in_specs=in_specs,
        out_specs=out_specs,
        out_shape=out_shape,
        compiler_params=_cparams("arbitrary"),
        name="resid_ln",
    )(*args)


def _mm_kernel(*refs, epilogue, n_extra, cast_b):
    a_ref, b_ref = refs[0], refs[1]
    extras = refs[2:2 + n_extra]
    o_ref = refs[2 + n_extra]
    if cast_b:
        wb_ref = refs[3 + n_extra]

        @pl.when(pl.program_id(1) == 0)
        def _():
            wb_ref[...] = b_ref[...].astype(BF16)

        b = wb_ref[...]
    else:
        b = b_ref[...]
    acc = jnp.dot(a_ref[...], b, preferred_element_type=F32)
    epilogue(acc, o_ref, *extras)


def _store(acc, o_ref):
    o_ref[...] = acc.astype(o_ref.dtype)


def _matmul(a, b, out_dtype, *, tn, n=None, col0=0, lead=None, epilogue=_store, extras=(),
            extra_specs=lambda row: [], name="matmul"):
    m, k = a.shape
    n = b.shape[-1] if n is None else n
    tm = MM_TM
    tn = min(tn, n)
    assert m % tm == 0 and n % tn == 0 and col0 % tn == 0
    cast_b = b.dtype != BF16
    jb0 = col0 // tn
    if lead is None:
        b_spec = pl.BlockSpec((k, tn), lambda j, i: (0, jb0 + j))
    else:
        b_spec = pl.BlockSpec((None, k, tn), lambda j, i: (lead, 0, jb0 + j))
    return pl.pallas_call(
        functools.partial(_mm_kernel, epilogue=epilogue, n_extra=len(extras), cast_b=cast_b),
        grid=(n // tn, m // tm),
        in_specs=[pl.BlockSpec((tm, k), lambda j, i: (i, 0)), b_spec, *extra_specs(lambda i: i)],
        out_specs=pl.BlockSpec((tm, tn), lambda j, i: (i, j)),
        out_shape=jax.ShapeDtypeStruct((m, n), out_dtype),
        scratch_shapes=[pltpu.VMEM((k, tn), BF16)] if cast_b else [],
        compiler_params=_cparams("arbitrary", "arbitrary"),
        name=name,
    )(a, b, *extras)


def _rope_tables(seq, n_ctx, rot_dim):
    rows = seq // GRID_W
    quarter = rot_dim // 4
    inv_freq = ROPE_THETA ** (-jnp.arange(quarter, dtype=F32) / quarter)
    ang_r = jnp.arange(rows, dtype=F32)[:, None] * inv_freq
    ang_c = jnp.arange(GRID_W, dtype=F32)[:, None] * inv_freq

    def per_token(fn):
        by_row = jnp.broadcast_to(fn(ang_r)[:, None, :], (rows, GRID_W, quarter))
        by_col = jnp.broadcast_to(fn(ang_c)[None, :, :], (rows, GRID_W, quarter))
        return jnp.concatenate([by_row, by_col], axis=-1).reshape(seq, 2 * quarter)

    cos, sin = per_token(jnp.cos), per_token(jnp.sin)
    half = rot_dim // 2
    pad = LANES - rot_dim
    zeros_h = jnp.zeros((seq, half), F32)
    zeros_p = jnp.zeros((seq, pad), F32)
    cos_t = jnp.concatenate([cos, cos, zeros_p], axis=-1)
    sin_lo = jnp.concatenate([-sin, zeros_h, zeros_p], axis=-1)
    sin_hi = jnp.concatenate([zeros_h, sin, zeros_p], axis=-1)
    ident = jnp.concatenate([jnp.ones((n_ctx, rot_dim), F32), jnp.zeros((n_ctx, pad), F32)], axis=-1)
    zc = jnp.zeros((n_ctx, LANES), F32)
    return (jnp.concatenate([cos_t, ident], axis=0), jnp.concatenate([sin_lo, zc], axis=0),
            jnp.concatenate([sin_hi, zc], axis=0))


def _rope_block(x, cos, sin_lo, sin_hi, half):
    return (x * cos + pltpu.roll(x, LANES - half, axis=1) * sin_lo
            + pltpu.roll(x, half, axis=1) * sin_hi)


def _head_epilogue(acc, o_ref, *extras, norm, rope, scale):
    idx = 0
    if norm:
        g = extras[idx][...]
        idx += 1
    if rope:
        cos, sin_lo, sin_hi = (r[...] for r in extras[idx:idx + 3])
    for hb in range(acc.shape[1] // LANES):
        x = acc[:, hb * LANES:(hb + 1) * LANES]
        if norm:
            x = x * lax.rsqrt(jnp.mean(x * x, axis=-1, keepdims=True) + RMS_EPS) * g
        if rope:
            x = _rope_block(x, cos, sin_lo, sin_hi, HEAD_DIM // 2)
        if scale != 1.0:
            x = x * scale
        o_ref[:, hb * LANES:(hb + 1) * LANES] = x.astype(o_ref.dtype)


def _vec_spec(n):
    return pl.BlockSpec((1, n), lambda j, i: (0, 0))


def _table_specs(row):
    return [pl.BlockSpec((MM_TM, LANES), lambda j, i: (row(i), 0))] * 3


def _head_proj(h, w, *, tn, n, col0, lead, norm_g=None, tables=None, scale=1.0, name):
    extras = ([norm_g.reshape(1, LANES)] if norm_g is not None else []) + list(tables or ())

    def specs(row):
        return (([_vec_spec(LANES)] if norm_g is not None else [])
                + (_table_specs(row) if tables is not None else []))

    ep = functools.partial(_head_epilogue, norm=norm_g is not None, rope=tables is not None, scale=scale)
    return _matmul(h, w, BF16, tn=tn, n=n, col0=col0, lead=lead, epilogue=ep, extras=extras,
                   extra_specs=specs, name=name)


def _rmsnorm_epilogue(acc, o_ref, g_ref):
    x = acc * lax.rsqrt(jnp.mean(acc * acc, axis=-1, keepdims=True) + RMS_EPS) * g_ref[...]
    o_ref[...] = x.astype(o_ref.dtype)


def _rope_m_epilogue(acc, o_ref, cos_ref, lo_ref, hi_ref):
    o_ref[...] = _rope_block(acc, cos_ref[...], lo_ref[...], hi_ref[...], MLA_ROPE // 2).astype(o_ref.dtype)


def _mla_q_epilogue(acc, o_ref, cos_ref, lo_ref, hi_ref, *, scale):
    cos, lo, hi = cos_ref[...], lo_ref[...], hi_ref[...]
    for hd in range(acc.shape[1] // MLA_HEAD_PAD):
        base = hd * MLA_HEAD_PAD
        o_ref[:, base:base + LANES] = (acc[:, base:base + LANES] * scale).astype(o_ref.dtype)
        r = _rope_block(acc[:, base + LANES:base + 2 * LANES], cos, lo, hi, MLA_ROPE // 2)
        o_ref[:, base + LANES:base + 2 * LANES] = (r * scale).astype(o_ref.dtype)


def _softmax_step(s, v, m_ref, l_ref, acc_ref, rows):
    m_old = m_ref[rows, :]
    m_new = jnp.maximum(m_old, jnp.max(s, axis=-1, keepdims=True))
    p = jnp.concatenate([jnp.exp2(s[:, j * LANES:(j + 1) * LANES] - m_new)
                         for j in range(s.shape[1] // LANES)], axis=1)
    alpha = jnp.exp2(m_old - m_new)
    if l_ref is not None:
        l_ref[rows, :] = alpha * l_ref[rows, :] + jnp.sum(p, axis=-1, keepdims=True)
    pv = jnp.dot(p.astype(BF16), v, preferred_element_type=F32)
    acc_ref[rows, :] = jnp.concatenate([alpha] * (acc_ref.shape[1] // LANES), axis=1) * acc_ref[rows, :] + pv
    m_ref[rows, :] = m_new


def _flash(q, k_ref, v_ref, m_ref, l_ref, acc_ref, s_ref, tk):
    n_chunks = k_ref.shape[0] // tk
    n_rows = q.shape[0]
    rb = min(FLASH_ROW_BLOCK, n_rows)
    blocks = [pl.ds(b * rb, rb) for b in range(n_rows // rb)]
    qs = [q[b * rb:(b + 1) * rb] for b in range(n_rows // rb)]
    m_ref[...] = jnp.full(m_ref.shape, -jnp.inf, F32)
    acc_ref[...] = jnp.zeros(acc_ref.shape, F32)
    if l_ref is not None:
        l_ref[...] = jnp.zeros(l_ref.shape, F32)
    nt_dims = (((1,), (1,)), ((), ()))
    s_ref[...] = lax.dot_general(q, k_ref[pl.ds(0, tk), :], nt_dims, preferred_element_type=F32)

    def body(c, carry):
        start = pl.multiple_of(c * tk, tk)
        nxt = pl.multiple_of((c + 1) * tk, tk)
        k_next = k_ref[pl.ds(nxt, tk), :]
        v = v_ref[pl.ds(start, tk), :]
        for rows, q_b in zip(blocks, qs):
            s = s_ref[rows, :]
            s_next = lax.dot_general(q_b, k_next, nt_dims, preferred_element_type=F32)
            _softmax_step(s, v, m_ref, l_ref, acc_ref, rows)
            s_ref[rows, :] = s_next
        return carry

    lax.fori_loop(0, n_chunks - 1, body, 0)
    v = v_ref[pl.ds((n_chunks - 1) * tk, tk), :]
    for rows in blocks:
        _softmax_step(s_ref[rows, :], v, m_ref, l_ref, acc_ref, rows)


def _ones_behind(v_ref, vaug_ref):
    vaug_ref[:, :LANES] = v_ref[...]
    vaug_ref[:, LANES:] = jnp.ones((v_ref.shape[0], LANES), BF16)


def _diff_attn_kernel(lam_ref, g_ref, q1_ref, q2_ref, k1_ref, k2_ref, v_ref, _prev_ref, o_ref,
                      m_ref, l_ref, acc_ref, s_ref, a1_ref, *, tk, lambda_init):
    lv = lam_ref[...]
    lam = (jnp.exp(jnp.sum(lv[0:1] * lv[1:2], axis=-1, keepdims=True))
           - jnp.exp(jnp.sum(lv[2:3] * lv[3:4], axis=-1, keepdims=True)) + lambda_init)
    n_rep = acc_ref.shape[1] // LANES
    _flash(q1_ref[...], k1_ref, v_ref, m_ref, l_ref, acc_ref, s_ref, tk)
    a1_ref[...] = acc_ref[...] / jnp.concatenate([l_ref[...]] * n_rep, axis=1)
    _flash(q2_ref[...], k2_ref, v_ref, m_ref, l_ref, acc_ref, s_ref, tk)
    o = a1_ref[...] - lam * (acc_ref[...] / jnp.concatenate([l_ref[...]] * n_rep, axis=1))
    o = o * lax.rsqrt(jnp.mean(o * o, axis=-1, keepdims=True) + RMS_EPS) * g_ref[...]
    o_ref[...] = (o * (1.0 - lambda_init)).astype(o_ref.dtype)


def _gqa_attn_kernel(q_ref, k_ref, v_ref, _prev_ref, o_ref, vaug_ref, qs_ref, m_ref, acc_ref, s_ref, *,
                     tk, rep):
    tq = q_ref.shape[0]

    @pl.when(pl.program_id(1) == 0)
    def _():
        _ones_behind(v_ref, vaug_ref)

    for r in range(rep):
        qs_ref[r * tq:(r + 1) * tq, :] = q_ref[:, r * LANES:(r + 1) * LANES]
    _flash(qs_ref[...], k_ref, vaug_ref, m_ref, None, acc_ref, s_ref, tk)
    for r in range(rep):
        rows = slice(r * tq, (r + 1) * tq)
        o_ref[:, r * LANES:(r + 1) * LANES] = (acc_ref[rows, :LANES] / acc_ref[rows, LANES:]).astype(o_ref.dtype)


def _mla_attn_kernel(q_ref, kn_ref, kr_ref, v_ref, _prev_ref, o_ref, kcat_ref, vaug_ref, m_ref, acc_ref,
                     s_ref, *, tk):
    @pl.when(pl.program_id(1) == 0)
    def _():
        kcat_ref[:, :LANES] = kn_ref[...]
        kcat_ref[:, LANES:] = kr_ref[...]
        _ones_behind(v_ref, vaug_ref)

    _flash(q_ref[...], kcat_ref, vaug_ref, m_ref, None, acc_ref, s_ref, tk)
    o_ref[...] = (acc_ref[:, :LANES] / acc_ref[:, LANES:]).astype(o_ref.dtype)


def _diff_attention(qa, ka, va, lam_vecs, subln_g, lambda_init, out, col0, *, q_rows, kv_rows, tq, tk):
    n_heads = va.shape[1] // (2 * HEAD_DIM)
    q0, nq = q_rows
    k0, nk = kv_rows
    qb, kb = q0 // tq, k0 // nk
    dv = 2 * HEAD_DIM
    cb = col0 // dv
    return pl.pallas_call(
        functools.partial(_diff_attn_kernel, tk=tk, lambda_init=lambda_init),
        grid=(n_heads, nq // tq),
        in_specs=[
            pl.BlockSpec((4, HEAD_DIM), lambda h, i: (0, 0)),
            pl.BlockSpec((1, dv), lambda h, i: (0, 0)),
            pl.BlockSpec((tq, HEAD_DIM), lambda h, i: (qb + i, 2 * h)),
            pl.BlockSpec((tq, HEAD_DIM), lambda h, i: (qb + i, 2 * h + 1)),
            pl.BlockSpec((nk, HEAD_DIM), lambda h, i: (kb, 2 * h)),
            pl.BlockSpec((nk, HEAD_DIM), lambda h, i: (kb, 2 * h + 1)),
            pl.BlockSpec((nk, dv), lambda h, i: (kb, h)),
            pl.BlockSpec(memory_space=pl.ANY),
        ],
        out_specs=pl.BlockSpec((tq, dv), lambda h, i: (qb + i, cb + h)),
        out_shape=jax.ShapeDtypeStruct(out.shape, out.dtype),
        input_output_aliases={7: 0},
        scratch_shapes=[pltpu.VMEM((tq, LANES), F32), pltpu.VMEM((tq, LANES), F32), pltpu.VMEM((tq, dv), F32),
                        pltpu.VMEM((tq, tk), F32), pltpu.VMEM((tq, dv), F32)],
        compiler_params=_cparams("arbitrary", "arbitrary"),
        name="diff_attention",
    )(lam_vecs, subln_g.reshape(1, dv), qa, qa, ka, ka, va, out)


def _gqa_attention(qg, kg, vg, out, col0, *, q_rows, kv_rows, tq, tk):
    n_kv = kg.shape[1] // HEAD_DIM
    rep = qg.shape[1] // kg.shape[1]
    q0, nq = q_rows
    k0, nk = kv_rows
    qb, kb = q0 // tq, k0 // nk
    rows = rep * tq
    cb = col0 // (rep * HEAD_DIM)
    return pl.pallas_call(
        functools.partial(_gqa_attn_kernel, tk=tk, rep=rep),
        grid=(n_kv, nq // tq),
        in_specs=[
            pl.BlockSpec((tq, rep * HEAD_DIM), lambda g, i: (qb + i, g)),
            pl.BlockSpec((nk, HEAD_DIM), lambda g, i: (kb, g)),
            pl.BlockSpec((nk, HEAD_DIM), lambda g, i: (kb, g)),
            pl.BlockSpec(memory_space=pl.ANY),
        ],
        out_specs=pl.BlockSpec((tq, rep * HEAD_DIM), lambda g, i: (qb + i, cb + g)),
        out_shape=jax.ShapeDtypeStruct(out.shape, out.dtype),
        input_output_aliases={3: 0},
        scratch_shapes=[pltpu.VMEM((nk, 2 * LANES), BF16), pltpu.VMEM((rows, HEAD_DIM), BF16),
                        pltpu.VMEM((rows, LANES), F32), pltpu.VMEM((rows, 2 * LANES), F32),
                        pltpu.VMEM((rows, tk), F32)],
        compiler_params=_cparams("arbitrary", "arbitrary"),
        name="gqa_attention",
    )(qg, kg, vg, out)


def _mla_attention(q, kv, kr, out, *, q_rows, kv_rows, tq, tk):
    n_heads = q.shape[1] // MLA_HEAD_PAD
    q0, nq = q_rows
    k0, nk = kv_rows
    qb, kb = q0 // tq, k0 // nk
    return pl.pallas_call(
        functools.partial(_mla_attn_kernel, tk=tk),
        grid=(n_heads, nq // tq),
        in_specs=[
            pl.BlockSpec((tq, MLA_HEAD_PAD), lambda h, i: (qb + i, h)),
            pl.BlockSpec((nk, MLA_NOPE), lambda h, i: (kb, 2 * h)),
            pl.BlockSpec((nk, LANES), lambda h, i: (kb, 0)),
            pl.BlockSpec((nk, MLA_V), lambda h, i: (kb, 2 * h + 1)),
            pl.BlockSpec(memory_space=pl.ANY),
        ],
        out_specs=pl.BlockSpec((tq, MLA_V), lambda h, i: (qb + i, h)),
        out_shape=jax.ShapeDtypeStruct(out.shape, out.dtype),
        input_output_aliases={4: 0},
        scratch_shapes=[pltpu.VMEM((nk, MLA_HEAD_PAD), BF16), pltpu.VMEM((nk, 2 * LANES), BF16),
                        pltpu.VMEM((tq, LANES), F32), pltpu.VMEM((tq, 2 * LANES), F32),
                        pltpu.VMEM((tq, tk), F32)],
        compiler_params=_cparams("arbitrary", "arbitrary"),
        name="mla_attention",
    )(q, kv, kr, kv, out)


def _route(logits):
    lane = lax.broadcasted_iota(jnp.int32, logits.shape, 1).astype(F32)
    neg = jnp.float32(-jnp.inf)
    big = jnp.float32(LANES)

    gmask = (lane >= N_EXPERTS) & (lane < N_EXPERTS + N_GROUPS)
    gl = jnp.where(gmask, logits, neg)
    ge = jnp.exp(gl - jnp.max(gl, axis=-1, keepdims=True))
    gprob = ge / jnp.sum(ge, axis=-1, keepdims=True)
    group_w = jnp.max(gprob, axis=-1, keepdims=True)
    gidx = jnp.min(jnp.where(gmask & (gprob == group_w), lane, big), axis=-1, keepdims=True) - N_EXPERTS

    emask = (lane >= gidx * EXPERTS_PER_GROUP) & (lane < (gidx + 1) * EXPERTS_PER_GROUP)
    el = jnp.where(emask, logits, neg)
    ee = jnp.exp(el - jnp.max(el, axis=-1, keepdims=True))
    ep = ee / jnp.sum(ee, axis=-1, keepdims=True)
    p1 = jnp.max(ep, axis=-1, keepdims=True)
    i1 = jnp.min(jnp.where(emask & (ep == p1), lane, big), axis=-1, keepdims=True)
    rest = emask & (lane != i1)
    p2 = jnp.max(jnp.where(rest, ep, neg), axis=-1, keepdims=True)
    i2 = jnp.min(jnp.where(rest & (ep == p2), lane, big), axis=-1, keepdims=True)
    denom = p1 + p2
    return (jnp.where(lane == i1, group_w * (p1 / denom), 0.0)
            + jnp.where(lane == i2, group_w * (p2 / denom), 0.0)
            + jnp.where(lane == N_EXPERTS, gidx, 0.0))


def _routing_plan(group, tm, n_tiles):
    m = group.shape[0]
    onehot = (group[:, None] == jnp.arange(N_GROUPS, dtype=jnp.int32)[None, :]).astype(jnp.int32)
    counts = jnp.sum(onehot, axis=0)
    padded = (counts + tm - 1) // tm * tm
    ends = jnp.cumsum(padded)
    starts = ends - padded
    rank = jnp.take_along_axis(jnp.cumsum(onehot, axis=0), group[:, None], axis=1)[:, 0] - 1
    slot = starts[group] + rank
    slot_token = jnp.zeros((n_tiles * tm,), jnp.int32).at[slot].set(jnp.arange(m, dtype=jnp.int32))
    tile_start = jnp.arange(n_tiles, dtype=jnp.int32) * tm
    tile_group = jnp.minimum(jnp.searchsorted(ends, tile_start, side="right"), N_GROUPS - 1)
    tile_rows = jnp.clip((starts + counts)[tile_group] - tile_start, 0, tm)
    return slot_token, tile_group.astype(jnp.int32), tile_rows.astype(jnp.int32)


def _for_rows(n_rows, tm, body):
    def step(r, carry):
        body(r)
        return carry

    @pl.when(n_rows == tm)
    def _():
        lax.fori_loop(0, tm, step, 0, unroll=ROW_DMA_UNROLL)

    @pl.when(n_rows < tm)
    def _():
        lax.fori_loop(0, n_rows, step, 0)


def _gather_rows_kernel(tok_ref, nrow_ref, x_ref, g_ref, xo_ref, go_ref, buf_ref, sems):
    t = pl.program_id(0)
    n_rows = nrow_ref[t]
    tm = buf_ref.shape[0]

    @pl.when(t == 0)
    def _():
        buf_ref[...] = jnp.zeros(buf_ref.shape, buf_ref.dtype)

    go_ref[...] = jnp.zeros(go_ref.shape, go_ref.dtype)

    def copies(r, tok):
        return (pltpu.make_async_copy(x_ref.at[pl.ds(tok, 1)], buf_ref.at[pl.ds(r, 1)], sems.at[0]),
                pltpu.make_async_copy(g_ref.at[pl.ds(tok, 1)], go_ref.at[pl.ds(r, 1)], sems.at[1]))

    def issue(r):
        for cp in copies(r, tok_ref[t * tm + r]):
            cp.start()

    def drain(r):
        for cp in copies(0, 0):
            cp.wait()

    _for_rows(n_rows, tm, issue)
    _for_rows(n_rows, tm, drain)
    xo_ref[...] = buf_ref[...].astype(xo_ref.dtype)


def _gather_rows(slot_token, tile_rows, x, gates):
    n_slots = slot_token.shape[0]
    d = x.shape[1]
    tm = MOE_TM
    any_spec = pl.BlockSpec(memory_space=pl.ANY)
    return pl.pallas_call(
        _gather_rows_kernel,
        grid_spec=pltpu.PrefetchScalarGridSpec(
            num_scalar_prefetch=2, grid=(n_slots // tm,),
            in_specs=[any_spec, any_spec],
            out_specs=[pl.BlockSpec((tm, d), lambda t, tok, nr: (t, 0)),
                       pl.BlockSpec((tm, LANES), lambda t, tok, nr: (t, 0))],
            scratch_shapes=[pltpu.VMEM((tm, d), x.dtype), pltpu.SemaphoreType.DMA((2,))]),
        out_shape=[jax.ShapeDtypeStruct((n_slots, d), BF16), jax.ShapeDtypeStruct((n_slots, LANES), F32)],
        compiler_params=_cparams("arbitrary"),
        name="moe_gather_rows",
    )(slot_token, tile_rows, x, gates)


def _new_weights(tg_ref, t):
    return jnp.logical_or(t == 0, tg_ref[t] != tg_ref[jnp.maximum(t - 1, 0)])


def _group_up_kernel(tg_ref, nrow_ref, x_ref, wga_ref, wgb_ref, wua_ref, wub_ref, gates_ref, o_ref,
                     wg_ref, wu_ref):
    nh, t = pl.program_id(0), pl.program_id(1)
    f = wga_ref.shape[1]

    @pl.when(_new_weights(tg_ref, t))
    def _():
        wg_ref[:, :f] = wga_ref[...].astype(BF16)
        wg_ref[:, f:] = wgb_ref[...].astype(BF16)
        wu_ref[:, :f] = wua_ref[...].astype(BF16)
        wu_ref[:, f:] = wub_ref[...].astype(BF16)

    @pl.when(nrow_ref[t] > 0)
    def _():
        x = x_ref[...]
        g = jnp.dot(x, wg_ref[...], preferred_element_type=F32)
        u = jnp.dot(x, wu_ref[...], preferred_element_type=F32)
        hid = g * jax.nn.sigmoid(g) * u
        gates = gates_ref[...]
        lane = lax.broadcasted_iota(jnp.int32, gates.shape, 1)
        for e in range(2):
            expert = tg_ref[t] * EXPERTS_PER_GROUP + nh * 2 + e
            gate = jnp.sum(jnp.where(lane == expert, gates, 0.0), axis=-1, keepdims=True)
            o_ref[:, e * f:(e + 1) * f] = (hid[:, e * f:(e + 1) * f] * gate).astype(o_ref.dtype)

    @pl.when(nrow_ref[t] == 0)
    def _():
        o_ref[...] = jnp.zeros(o_ref.shape, o_ref.dtype)


def _group_up(tile_group, tile_rows, xs, w_gate, w_up, gates_s, layer):
    mp, d = xs.shape
    f = w_gate.shape[3]
    tm = MOE_TM

    def wspec(e):
        return pl.BlockSpec((None, None, d, f),
                            lambda nh, t, tg, nr: (layer, tg[t] * EXPERTS_PER_GROUP + nh * 2 + e, 0, 0),
                            pipeline_mode=pl.Buffered(1))

    return pl.pallas_call(
        _group_up_kernel,
        grid_spec=pltpu.PrefetchScalarGridSpec(
            num_scalar_prefetch=2, grid=(EXPERTS_PER_GROUP // 2, mp // tm),
            in_specs=[pl.BlockSpec((tm, d), lambda nh, t, tg, nr: (t, 0)),
                      wspec(0), wspec(1), wspec(0), wspec(1),
                      pl.BlockSpec((tm, LANES), lambda nh, t, tg, nr: (t, 0))],
            out_specs=pl.BlockSpec((tm, 2 * f), lambda nh, t, tg, nr: (t, nh)),
            scratch_shapes=[pltpu.VMEM((d, 2 * f), BF16), pltpu.VMEM((d, 2 * f), BF16)]),
        out_shape=jax.ShapeDtypeStruct((mp, EXPERTS_PER_GROUP * f), BF16),
        compiler_params=_cparams("arbitrary", "arbitrary"),
        name="moe_group_up",
    )(tile_group, tile_rows, xs, w_gate, w_gate, w_up, w_up, gates_s)


def _group_down_kernel(tg_ref, tok_ref, nrow_ref, h_ref, w_ref, y_ref, wb_ref, buf_ref, sems):
    t = pl.program_id(0)
    n_rows = nrow_ref[t]
    tm = buf_ref.shape[1]
    slot = t % 2

    def copy(s, r, tok):
        return pltpu.make_async_copy(buf_ref.at[s, pl.ds(r, 1)], y_ref.at[pl.ds(tok, 1)], sems.at[s])

    def wait_rows(s, count):
        _for_rows(count, tm, lambda r: copy(s, 0, 0).wait())

    @pl.when(_new_weights(tg_ref, t))
    def _():
        wb_ref[...] = w_ref[...].astype(BF16)

    @pl.when(n_rows > 0)
    def _():
        buf_ref[slot] = jnp.dot(h_ref[...], wb_ref[...], preferred_element_type=F32)
        _for_rows(n_rows, tm, lambda r: copy(slot, r, tok_ref[t * tm + r]).start())

    @pl.when(t > 0)
    def _():
        wait_rows(1 - slot, nrow_ref[jnp.maximum(t - 1, 0)])

    @pl.when(t == pl.num_programs(0) - 1)
    def _():
        wait_rows(slot, n_rows)


def _group_down(tile_group, slot_token, tile_rows, hidden_s, w_down_g, m, layer):
    mp, n_wide = hidden_s.shape
    d = w_down_g.shape[3]
    tm = MOE_TM
    return pl.pallas_call(
        _group_down_kernel,
        grid_spec=pltpu.PrefetchScalarGridSpec(
            num_scalar_prefetch=3, grid=(mp // tm,),
            in_specs=[pl.BlockSpec((tm, n_wide), lambda t, tg, tok, nr: (t, 0)),
                      pl.BlockSpec((None, None, n_wide, d), lambda t, tg, tok, nr: (layer, tg[t], 0, 0),
                                   pipeline_mode=pl.Buffered(1))],
            out_specs=pl.BlockSpec(memory_space=pl.ANY),
            scratch_shapes=[pltpu.VMEM((n_wide, d), BF16), pltpu.VMEM((2, tm, d), F32),
                            pltpu.SemaphoreType.DMA((2,))]),
        out_shape=jax.ShapeDtypeStruct((m, d), F32),
        compiler_params=_cparams("arbitrary"),
        name="moe_group_down",
    )(tile_group, slot_token, tile_rows, hidden_s, w_down_g)


def _routed_experts(h2, gates, w_gate, w_up, w_down, layer):
    m, d = h2.shape
    depth, n_exp, _, f = w_gate.shape
    tm = MOE_TM
    n_tiles = m // tm + N_GROUPS
    group = gates[:, N_EXPERTS].astype(jnp.int32)
    slot_token, tile_group, tile_rows = _routing_plan(group, tm, n_tiles)
    xs, gates_s = _gather_rows(slot_token, tile_rows, h2, gates)
    hidden_s = _group_up(tile_group, tile_rows, xs, w_gate, w_up, gates_s, layer)
    return _group_down(tile_group, slot_token, tile_rows, hidden_s,
                       w_down.reshape(depth, N_GROUPS, EXPERTS_PER_GROUP * f, d), m, layer)


def kernel(x, c, ctx, c_ctx, w_ada, b_ada, ln1_g, ln1_b, ln2_g, ln2_b, ev_w_in, ev_w_out, diff_lambda, diff_subln_g, gqa_q_norm_g, gqa_k_norm_g, od_w_in, mla_q_norm_g, mla_kv_norm_g, mla_w_uq, mla_w_ukv, od_w_out, moe_w_group, moe_b_group, moe_w_router, moe_b_router, moe_w_gate, moe_w_up, moe_w_down):
    batch, seq, d = x.shape
    n_ctx = ctx.shape[1]
    depth = w_ada.shape[0]
    assert batch == 1 and seq % ATTN_ROWS == 0 and n_ctx == ROW_TILE and (seq + n_ctx) % MM_TM == 0
    m = seq + n_ctx
    alpha = (2.0 * depth) ** 0.25
    lat_rows, ctx_rows = (0, seq), (seq, n_ctx)
    all_rows = (0, m)

    xs = jnp.concatenate([x.reshape(seq, d), ctx.reshape(n_ctx, d)], axis=0)
    mods3 = _ada_mods(c, c_ctx, w_ada, b_ada).reshape(depth * 2 * N_MOD, 1, d)
    tab_a = _rope_tables(seq, n_ctx, HEAD_DIM)
    tab_m = _rope_tables(seq, n_ctx, MLA_ROPE)

    h = _modulate(xs, mods3, 0, 1, 0, seq)
    for layer in range(depth):
        i = layer // 2
        if layer % 2 == 0:
            lambda_init = 0.8 - 0.6 * math.exp(-0.3 * layer)
            scale = LOG2E / math.sqrt(HEAD_DIM)
            n_diff_qk = d // 2
            n_diff_v = d // 2
            n_gqa_q = d // 2
            n_gqa_kv = n_gqa_q // 4
            widths = (n_diff_qk, n_diff_qk, n_diff_v, n_gqa_q, n_gqa_kv, n_gqa_kv)
            c0 = [sum(widths[:s]) for s in range(len(widths))]
            seg = lambda s: dict(tn=MM_TN, n=widths[s], col0=c0[s], lead=i)
            qa = _head_proj(h, ev_w_in, **seg(0), tables=tab_a, scale=scale, name="proj_diff_q")
            ka = _head_proj(h, ev_w_in, **seg(1), tables=tab_a, name="proj_diff_k")
            va = _matmul(h, ev_w_in, BF16, **seg(2), name="proj_diff_v")
            qg = _head_proj(h, ev_w_in, **seg(3), norm_g=gqa_q_norm_g[i], tables=tab_a, scale=scale,
                            name="proj_gqa_q")
            kg = _head_proj(h, ev_w_in, **seg(4), norm_g=gqa_k_norm_g[i], tables=tab_a, name="proj_gqa_k")
            vg = _matmul(h, ev_w_in, BF16, **seg(5), name="proj_gqa_v")
            attn = jnp.zeros((m, d), BF16)
            for q_rows, kv_rows, tq_d, tq_g, tk in ((lat_rows, all_rows, ATTN_ROWS, ATTN_ROWS // 4, KV_CHUNK),
                                                    (ctx_rows, ctx_rows, n_ctx, n_ctx, n_ctx)):
                attn = _diff_attention(qa, ka, va, diff_lambda[i], diff_subln_g[i], lambda_init, attn, 0,
                                       q_rows=q_rows, kv_rows=kv_rows, tq=tq_d, tk=tk)
                attn = _gqa_attention(qg, kg, vg, attn, n_diff_v, q_rows=q_rows, kv_rows=kv_rows, tq=tq_g,
                                      tk=tk)
            w_out = ev_w_out
        else:
            scale = LOG2E / math.sqrt(MLA_NOPE + MLA_ROPE)
            w_in = od_w_in[i]
            q_rank = mla_q_norm_g.shape[1]
            kv_rank = mla_kv_norm_g.shape[1]
            n_heads = mla_w_uq.shape[2] // (MLA_NOPE + MLA_ROPE)
            cq = _matmul(h, w_in[:, :q_rank].astype(BF16), BF16, tn=q_rank, epilogue=_rmsnorm_epilogue,
                         extras=[mla_q_norm_g[i].reshape(1, q_rank)],
                         extra_specs=lambda row: [_vec_spec(q_rank)], name="proj_mla_cq")
            ckv = _matmul(h, od_w_in, BF16, tn=kv_rank, n=kv_rank, col0=q_rank, lead=i,
                          epilogue=_rmsnorm_epilogue, extras=[mla_kv_norm_g[i].reshape(1, kv_rank)],
                          extra_specs=lambda row: [_vec_spec(kv_rank)], name="proj_mla_ckv")
            w_kr = jnp.pad(w_in[:, q_rank + kv_rank:], ((0, 0), (0, LANES - MLA_ROPE))).astype(BF16)
            kr = _matmul(h, w_kr, BF16, tn=LANES, epilogue=_rope_m_epilogue, extras=tab_m,
                         extra_specs=_table_specs, name="proj_mla_krope")
            w_uq = jnp.pad(mla_w_uq[i].reshape(q_rank, n_heads, MLA_NOPE + MLA_ROPE),
                           ((0, 0), (0, 0), (0, MLA_HEAD_PAD - MLA_NOPE - MLA_ROPE)))
            w_uq = w_uq.reshape(q_rank, n_heads * MLA_HEAD_PAD).astype(BF16)
            q = _matmul(cq, w_uq, BF16, tn=MM_TN_WIDE, epilogue=functools.partial(_mla_q_epilogue, scale=scale),
                        extras=tab_m, extra_specs=_table_specs, name="proj_mla_q")
            kv = _matmul(ckv, mla_w_ukv, BF16, tn=MM_TN_WIDE, lead=i, name="proj_mla_kv")
            attn = jnp.zeros((m, n_heads * MLA_V), BF16)
            attn = _mla_attention(q, kv, kr, attn, q_rows=lat_rows, kv_rows=all_rows, tq=ATTN_ROWS, tk=KV_CHUNK)
            attn = _mla_attention(q, kv, kr, attn, q_rows=ctx_rows, kv_rows=ctx_rows, tq=n_ctx, tk=n_ctx)
            w_out = od_w_out

        o = _matmul(attn, w_out, F32, tn=MM_TN, lead=i, name="proj_out")
        w_rt = jnp.concatenate([moe_w_router[layer], moe_w_group[layer],
                                jnp.zeros((d, LANES - N_EXPERTS - N_GROUPS), F32)], axis=1).astype(BF16)
        b_rt = jnp.concatenate([moe_b_router[layer], moe_b_group[layer],
                                jnp.zeros((LANES - N_EXPERTS - N_GROUPS,), F32)]).reshape(1, LANES)
        xs, h2, gates = _resid_ln(xs, o, mods3, layer, 2, ln1_g[layer], ln1_b[layer], layer, 4, 3, seq, alpha,
                                  h_dtype=F32, router=(w_rt, b_rt))
        y = _routed_experts(h2, gates, moe_w_gate, moe_w_up, moe_w_down, layer)
        nxt = min(layer + 1, depth - 1)
        last = layer == depth - 1
        xs, h = _resid_ln(xs, y, mods3, layer, 5, ln2_g[layer], ln2_b[layer], nxt, 1, 0, seq, alpha,
                          n_rows=seq if last else None)
    return xs.reshape(batch, seq, d)
```
